```python
import math
import jax
import jax.numpy as jnp
from jax import lax
import numpy as np

D_MODEL = 2048
BATCH = 4
SEQ = 4096
DEPTH = 4

GRID_W = 64
CTX_LEN = 256
HEAD_DIM = 128
NA_HEADS = 4
NA_WIN_R = 8
NA_WIN_C = 16
NA_QBLOCK_C = 16
NA_KBLOCK_C = 32
GQA_Q_HEADS = 8
GQA_KV_HEADS = 2
ROPE_THETA = 10000.0
Q_BLOCK = 128
DN_HEADS = 4
DN_CONV = 5
DN_CHUNK = 64
W_A = NA_HEADS * HEAD_DIM
W_B = GQA_Q_HEADS * HEAD_DIM
W_KV = GQA_KV_HEADS * HEAD_DIM
W_C = DN_HEADS * HEAD_DIM
MIX_WIDTH = W_A + W_B + W_C
IN_SPLITS = (W_A, W_A, W_A, W_B, W_KV, W_KV, W_C, W_C, W_C, W_C, 2 * DN_HEADS, 2 * DN_HEADS)
D_IN = sum(IN_SPLITS)
IN_OFFSETS = tuple(sum(IN_SPLITS[:i + 1]) for i in range(len(IN_SPLITS) - 1))
N_EXPERTS = 16
EC_FACTOR = 2
D_EXPERT = 1024
N_MOD = 6
NORM_EPS = 1e-6
NEG_INF = -1e30
F32 = jnp.float32

kernel_name = 'hybrid_diffusion_trunk'


def rms_norm(x, gain):
    xf = x.astype(F32)
    y = xf * lax.rsqrt(jnp.mean(xf * xf, axis=-1, keepdims=True) + NORM_EPS)
    return (y * gain.astype(F32)).astype(x.dtype)


def ada_norm(x, gain, shift, scale):
    return rms_norm(x, gain) * (1 + scale) + shift


def l2_normalize(x):
    xf = x.astype(F32)
    return (xf * lax.rsqrt(jnp.sum(xf * xf, axis=-1, keepdims=True) + NORM_EPS)).astype(x.dtype)


def split_heads(t, n_heads):
    return t.reshape(t.shape[0], t.shape[1], n_heads, t.shape[-1] // n_heads)


def axial_rope_tables(n_tok):
    t = jnp.arange(n_tok, dtype=jnp.int32)
    pos = jnp.stack([t // GRID_W, t % GRID_W], axis=-1).astype(F32)
    n_freq = HEAD_DIM // 4
    freqs = ROPE_THETA ** (-jnp.arange(n_freq, dtype=F32) / n_freq)
    ang = pos[:, :, None] * freqs
    return jnp.cos(ang), jnp.sin(ang)


def apply_axial_rope(x, rope):
    cos, sin = rope
    B, N, H, Dh = x.shape
    xf = x.astype(F32).reshape(B, N, H, 2, 2, Dh // 4)
    x1, x2 = xf[..., 0, :], xf[..., 1, :]
    c, s = cos[None, :, None], sin[None, :, None]
    out = jnp.stack([x1 * c - x2 * s, x1 * s + x2 * c], axis=-2)
    return out.reshape(B, N, H, Dh).astype(x.dtype)


def attend(q, k, v):
    B, Q, Hq, Dh = q.shape
    Hkv = k.shape[2]
    qg = q.reshape(B, Q, Hkv, Hq // Hkv, Dh)
    s = jnp.einsum('bqhgd,bkhd->bhgqk', qg, k).astype(F32) * (Dh ** -0.5)
    p = jax.nn.softmax(s, axis=-1).astype(v.dtype)
    return jnp.einsum('bhgqk,bkhd->bqhgd', p, v).reshape(B, Q, Hq * Dh)


def blocked_attention(q, k, v):
    B, N, Hq, Dh = q.shape
    qb = jnp.moveaxis(q.reshape(B, N // Q_BLOCK, Q_BLOCK, Hq, Dh), 1, 0)
    o = lax.map(lambda q_blk: attend(q_blk, k, v), qb)
    return jnp.moveaxis(o, 0, 1).reshape(B, N, Hq * Dh)


def neighbourhood_attention(q, k, v, k_ctx, v_ctx, rpb):
    B, N, H, Dh = q.shape
    rows = N // GRID_W
    wr = min(NA_WIN_R, rows)
    scale = Dh ** -0.5
    grid = lambda t: t.reshape(B, rows, GRID_W, H, Dh)
    qg, kg, vg = grid(q), grid(k), grid(v)
    r = jnp.arange(rows)
    row_idx = jnp.clip(r - wr // 2, 0, rows - wr)[:, None] + jnp.arange(wr)[None, :]
    dr_idx = row_idx - r[:, None] + (NA_WIN_R - 1)
    n_loc = wr * NA_KBLOCK_C
    outs = []
    for c0 in range(0, GRID_W, NA_QBLOCK_C):
        kc0 = min(max(c0 - NA_WIN_C // 2, 0), GRID_W - NA_KBLOCK_C)
        qcol = jnp.arange(c0, c0 + NA_QBLOCK_C)
        kcol = jnp.arange(kc0, kc0 + NA_KBLOCK_C)
        cstart = jnp.clip(qcol - NA_WIN_C // 2, 0, GRID_W - NA_WIN_C)
        in_win = (kcol[None, :] >= cstart[:, None]) & (kcol[None, :] < cstart[:, None] + NA_WIN_C)
        dc_idx = jnp.clip(kcol[None, :] - qcol[:, None] + (NA_WIN_C - 1), 0, 2 * NA_WIN_C - 2)
        bias = rpb[:, dr_idx[:, None, :, None], dc_idx[None, :, None, :]]
        q_blk = qg[:, :, c0:c0 + NA_QBLOCK_C]
        k_blk = kg[:, :, kc0:kc0 + NA_KBLOCK_C][:, row_idx]
        v_blk = vg[:, :, kc0:kc0 + NA_KBLOCK_C][:, row_idx]
        s_loc = jnp.einsum('brqhd,brwkhd->bhrqwk', q_blk, k_blk).astype(F32) * scale + bias[None].astype(F32)
        s_loc = jnp.where(in_win[:, None, :], s_loc, NEG_INF).reshape(B, H, rows, NA_QBLOCK_C, n_loc)
        s_ctx = jnp.einsum('brqhd,bkhd->bhrqk', q_blk, k_ctx).astype(F32) * scale
        p = jax.nn.softmax(jnp.concatenate([s_loc, s_ctx], axis=-1), axis=-1).astype(v.dtype)
        o = (jnp.einsum('bhrqn,brnhd->brqhd', p[..., :n_loc], v_blk.reshape(B, rows, n_loc, H, Dh))
             + jnp.einsum('bhrqk,bkhd->brqhd', p[..., n_loc:], v_ctx))
        outs.append(o)
    return jnp.concatenate(outs, axis=2).reshape(B, N, H * Dh)


def short_conv(x, w):
    return lax.conv_general_dilated(x, w[:, None, :].astype(x.dtype), window_strides=(1,),
                                    padding=[(DN_CONV // 2, DN_CONV // 2)],
                                    dimension_numbers=('NWC', 'WIO', 'NWC'),
                                    feature_group_count=x.shape[-1])


def deltanet_inputs(q, k, v, beta_raw, dec_raw, conv_w, a_log, dt_bias, with_q):
    B, N, _ = k.shape
    wq, wk, wv = jnp.split(conv_w, 3, axis=-1)
    conv_heads = lambda t, w: split_heads(jax.nn.silu(short_conv(t, w)), DN_HEADS)
    k = l2_normalize(conv_heads(k, wk))
    v = conv_heads(v, wv)
    q = l2_normalize(conv_heads(q, wq)) * (HEAD_DIM ** -0.5) if with_q else None
    beta = jax.nn.sigmoid(beta_raw.astype(F32)).reshape(B, N, 2, DN_HEADS)
    g = -jnp.exp(a_log.astype(F32)) * jax.nn.softplus(
        dec_raw.astype(F32).reshape(B, N, 2, DN_HEADS) + dt_bias.astype(F32))
    return q, k, v, beta, g


def delta_rule_chunked(q, k, v, beta, g, s0, with_output):
    B, N, H, Dk = k.shape
    Dv = v.shape[-1]
    v_dtype = v.dtype
    L = DN_CHUNK

    def chunks(t):
        t = t.astype(F32).reshape(B, N // L, L, H, *t.shape[3:])
        return jnp.moveaxis(t, 3, 1)

    k, v, beta, g = chunks(k), chunks(v), chunks(beta), chunks(g)
    G = jnp.cumsum(g, axis=-1)
    incl = jnp.tril(jnp.ones((L, L), dtype=bool))
    decay = jnp.exp(jnp.where(incl, G[..., :, None] - G[..., None, :], -jnp.inf))
    k_beta = k * beta[..., None]
    a_strict = jnp.tril(jnp.einsum('bhnid,bhnjd->bhnij', k_beta, k) * decay, -1)
    rhs = jnp.concatenate([v * beta[..., None], k_beta * jnp.exp(G)[..., None]], axis=-1)
    sol = lax.linalg.triangular_solve(a_strict + jnp.eye(L, dtype=F32), rhs,
                                      left_side=True, lower=True, unit_diagonal=True)
    u, w = sol[..., :Dv], sol[..., Dv:]
    g_last = G[..., -1]
    k_tail = k * jnp.exp(g_last[..., None] - G)[..., None]
    xs = [u, w, k_tail, jnp.exp(g_last)]
    if with_output:
        qf = chunks(q)
        xs = xs + [qf * jnp.exp(G)[..., None], jnp.einsum('bhnid,bhnjd->bhnij', qf, k) * decay]
    xs = [jnp.moveaxis(t, 2, 0) for t in xs]

    def step(S, xc):
        u_c, w_c, kt_c, gl_c = xc[:4]
        v_new = u_c - jnp.einsum('bhld,bhde->bhle', w_c, S)
        S_next = S * gl_c[..., None, None] + jnp.einsum('bhld,bhle->bhde', kt_c, v_new)
        if not with_output:
            return S_next, None
        qh_c, qk_c = xc[4:]
        o = jnp.einsum('bhld,bhde->bhle', qh_c, S) + jnp.einsum('bhlm,bhme->bhle', qk_c, v_new)
        return S_next, o

    s_final, o = lax.scan(step, s0, xs)
    if with_output:
        o = jnp.transpose(o, (1, 0, 3, 2, 4)).reshape(B, N, H, Dv).astype(v_dtype)
    return s_final, o


def deltanet_bidir(q, k, v, beta, g, s0_f, s0_b, with_output):
    rev = lambda t: None if t is None else jnp.flip(t, axis=1)
    s_f, o_f = delta_rule_chunked(q, k, v, beta[:, :, 0], g[:, :, 0], s0_f, with_output)
    s_b, o_b = delta_rule_chunked(rev(q), rev(k), rev(v), rev(beta[:, :, 1]), rev(g[:, :, 1]), s0_b, with_output)
    o = o_f + rev(o_b) if with_output else None
    return s_f, s_b, o


def deltanet_output(o, gate, out_gain):
    o = rms_norm(o, out_gain) * jax.nn.silu(split_heads(gate, DN_HEADS))
    return o.reshape(o.shape[0], o.shape[1], -1)


def expert_choice_ffn(h, router_w, w_gate, w_up, w_down):
    B, N, _ = h.shape
    cap = (EC_FACTOR * N) // N_EXPERTS
    aff = jax.nn.softmax(jnp.einsum('bnd,de->bne', h, router_w).astype(F32), axis=-1)
    gate, idx = lax.top_k(jnp.swapaxes(aff, 1, 2), cap)
    b_idx = jnp.arange(B)[:, None, None]
    xe = h[b_idx, idx]
    hid = jax.nn.silu(jnp.einsum('becd,edf->becf', xe, w_gate)) * jnp.einsum('becd,edf->becf', xe, w_up)
    ye = jnp.einsum('becf,efd->becd', hid, w_down) * gate[..., None].astype(h.dtype)
    return jnp.zeros_like(h).at[b_idx, idx].add(ye)


def hybrid_layer(x, ctx, c, c_ctx, rope, ada_w, ada_b, g_mix, g_ffn, w_in, w_out, na_qk_gain, na_rpb,
                 gqa_qk_gain, dn_conv, dn_a_log, dn_dt_bias, dn_out_gain, router_w, w_gate, w_up, w_down,
                 need_ctx_out):
    sh_l, sc_l, gt_l, sh2_l, sc2_l, gt2_l = jnp.split((jax.nn.silu(c) @ ada_w + ada_b)[:, None, :], N_MOD, axis=-1)
    sh_c, sc_c, gt_c, sh2_c, sc2_c, gt2_c = jnp.split(jax.nn.silu(c_ctx) @ ada_w + ada_b, N_MOD, axis=-1)

    p_lat = jnp.split(ada_norm(x, g_mix, sh_l, sc_l) @ w_in, IN_OFFSETS, axis=-1)
    p_ctx = jnp.split(ada_norm(ctx, g_mix, sh_c, sc_c) @ w_in, IN_OFFSETS, axis=-1)
    qa_l, ka_l, va_l, qb_l, kb_l, vb_l, qc_l, kc_l, vc_l, gc_l, beta_l, dec_l = p_lat
    qa_c, ka_c, va_c, qb_c, kb_c, vb_c, qc_c, kc_c, vc_c, gc_c, beta_c, dec_c = p_ctx

    k_na_c = rms_norm(split_heads(ka_c, NA_HEADS), na_qk_gain[1])
    v_na_c = split_heads(va_c, NA_HEADS)
    oa_lat = neighbourhood_attention(rms_norm(split_heads(qa_l, NA_HEADS), na_qk_gain[0]),
                                     rms_norm(split_heads(ka_l, NA_HEADS), na_qk_gain[1]),
                                     split_heads(va_l, NA_HEADS), k_na_c, v_na_c, na_rpb)

    k_g_c = rms_norm(split_heads(kb_c, GQA_KV_HEADS), gqa_qk_gain[1])
    v_g_c = split_heads(vb_c, GQA_KV_HEADS)
    q_g = apply_axial_rope(rms_norm(split_heads(qb_l, GQA_Q_HEADS), gqa_qk_gain[0]), rope)
    k_g = apply_axial_rope(rms_norm(split_heads(kb_l, GQA_KV_HEADS), gqa_qk_gain[1]), rope)
    ob_lat = blocked_attention(q_g, jnp.concatenate([k_g_c, k_g], axis=1),
                               jnp.concatenate([v_g_c, split_heads(vb_l, GQA_KV_HEADS)], axis=1))

    ctx_dn = deltanet_inputs(qc_c, kc_c, vc_c, beta_c, dec_c, dn_conv, dn_a_log, dn_dt_bias, need_ctx_out)
    lat_dn = deltanet_inputs(qc_l, kc_l, vc_l, beta_l, dec_l, dn_conv, dn_a_log, dn_dt_bias, True)
    s0 = jnp.zeros((x.shape[0], DN_HEADS, HEAD_DIM, HEAD_DIM), F32)
    s_cf, s_cb, o_dn_c = deltanet_bidir(*ctx_dn, s0, s0, need_ctx_out)
    _, _, o_dn_l = deltanet_bidir(*lat_dn, s_cf, s_cb, True)
    oc_lat = deltanet_output(o_dn_l, gc_l, dn_out_gain)

    x = x + gt_l * (jnp.concatenate([oa_lat, ob_lat, oc_lat], axis=-1) @ w_out)
    x = x + gt2_l * expert_choice_ffn(ada_norm(x, g_ffn, sh2_l, sc2_l), router_w, w_gate, w_up, w_down)
    if not need_ctx_out:
        return x, None

    oa_ctx = attend(rms_norm(split_heads(qa_c, NA_HEADS), na_qk_gain[0]), k_na_c, v_na_c)
    ob_ctx = attend(rms_norm(split_heads(qb_c, GQA_Q_HEADS), gqa_qk_gain[0]), k_g_c, v_g_c)
    oc_ctx = deltanet_output(o_dn_c, gc_c, dn_out_gain)
    ctx = ctx + gt_c * (jnp.concatenate([oa_ctx, ob_ctx, oc_ctx], axis=-1) @ w_out)
    ctx = ctx + gt2_c * expert_choice_ffn(ada_norm(ctx, g_ffn, sh2_c, sc2_c), router_w, w_gate, w_up, w_down)
    return x, ctx


def setup_inputs(seed: int = 0) -> dict:
    key = jax.random.key(seed)
    ks = jax.random.split(key, 24)
    nrm = lambda k, shape, s: jax.random.normal(k, shape, F32) * s
    dt = jnp.exp(jax.random.uniform(ks[15], (DEPTH, 2, DN_HEADS), F32, math.log(1e-3), math.log(1e-1)))
    return {
        'x': nrm(ks[0], (BATCH, SEQ, D_MODEL), 1.0),
        'c': nrm(ks[1], (BATCH, D_MODEL), 1.0),
        'ctx': nrm(ks[2], (BATCH, CTX_LEN, D_MODEL), 1.0),
        'c_ctx': nrm(ks[3], (D_MODEL,), 1.0),
        'ada_w': nrm(ks[4], (DEPTH, D_MODEL, N_MOD * D_MODEL), 0.5 * D_MODEL ** -0.5),
        'ada_b': nrm(ks[5], (DEPTH, N_MOD * D_MODEL), 0.02),
        'norm_mix': 1.0 + nrm(ks[6], (DEPTH, D_MODEL), 0.02),
        'norm_ffn': 1.0 + nrm(ks[7], (DEPTH, D_MODEL), 0.02),
        'w_in': nrm(ks[8], (DEPTH, D_MODEL, D_IN), D_MODEL ** -0.5),
        'w_out': nrm(ks[9], (DEPTH, MIX_WIDTH, D_MODEL), MIX_WIDTH ** -0.5),
        'na_qk_gain': 1.0 + nrm(ks[10], (DEPTH, 2, HEAD_DIM), 0.02),
        'na_rpb': nrm(ks[11], (DEPTH, NA_HEADS, 2 * NA_WIN_R - 1, 2 * NA_WIN_C - 1), 0.1),
        'gqa_qk_gain': 1.0 + nrm(ks[12], (DEPTH, 2, HEAD_DIM), 0.02),
        'dn_conv': nrm(ks[13], (DEPTH, DN_CONV, 3 * W_C), DN_CONV ** -0.5),
        'dn_a_log': jnp.log(jax.random.uniform(ks[14], (DEPTH, 2, DN_HEADS), F32, 1.0, 16.0)),
        'dn_dt_bias': dt + jnp.log(-jnp.expm1(-dt)),
        'dn_out_gain': 1.0 + nrm(ks[16], (DEPTH, HEAD_DIM), 0.02),
        'router_w': nrm(ks[17], (DEPTH, D_MODEL, N_EXPERTS), D_MODEL ** -0.5),
        'exp_w_gate': nrm(ks[18], (DEPTH, N_EXPERTS, D_MODEL, D_EXPERT), D_MODEL ** -0.5),
        'exp_w_up': nrm(ks[19], (DEPTH, N_EXPERTS, D_MODEL, D_EXPERT), D_MODEL ** -0.5),
        'exp_w_down': nrm(ks[20], (DEPTH, N_EXPERTS, D_EXPERT, D_MODEL), D_EXPERT ** -0.5),
    }


def reference(x, c, ctx, c_ctx, ada_w, ada_b, norm_mix, norm_ffn, w_in, w_out, na_qk_gain, na_rpb,
              gqa_qk_gain, dn_conv, dn_a_log, dn_dt_bias, dn_out_gain, router_w, exp_w_gate, exp_w_up,
              exp_w_down):
    rope = axial_rope_tables(x.shape[1])
    for i in range(DEPTH):
        x, ctx = hybrid_layer(
            x, ctx, c, c_ctx, rope,
            ada_w=ada_w[i], ada_b=ada_b[i], g_mix=norm_mix[i], g_ffn=norm_ffn[i],
            w_in=w_in[i], w_out=w_out[i], na_qk_gain=na_qk_gain[i], na_rpb=na_rpb[i],
            gqa_qk_gain=gqa_qk_gain[i], dn_conv=dn_conv[i], dn_a_log=dn_a_log[i],
            dn_dt_bias=dn_dt_bias[i], dn_out_gain=dn_out_gain[i], router_w=router_w[i],
            w_gate=exp_w_gate[i], w_up=exp_w_up[i], w_down=exp_w_down[i],
            need_ctx_out=(i < DEPTH - 1))
    return x
```

```python
import functools
import math

import jax
import jax.numpy as jnp
from jax import lax
from jax.experimental import pallas as pl
from jax.experimental.pallas import tpu as pltpu

D_MODEL = 2048
SEQ = 4096
DEPTH = 4
GRID_W = 64
CTX_LEN = 256
HEAD_DIM = 128
NA_HEADS = 4
NA_WIN_R = 8
NA_WIN_C = 16
GQA_Q_HEADS = 8
GQA_KV_HEADS = 2
ROPE_THETA = 10000.0
DN_HEADS = 4
DN_CONV = 5
W_A = NA_HEADS * HEAD_DIM
W_B = GQA_Q_HEADS * HEAD_DIM
W_KV = GQA_KV_HEADS * HEAD_DIM
W_C = DN_HEADS * HEAD_DIM
MIX_WIDTH = W_A + W_B + W_C
D_MAIN = 3 * W_A + W_B + 2 * W_KV + 4 * W_C
N_EXPERTS = 16
EC_FACTOR = 2
D_EXPERT = 1024
N_MOD = 6
NORM_EPS = 1e-6
NEG_INF = -1e30
F32 = jnp.float32
BF16 = jnp.bfloat16

LANES = 128
VMEM_LIMIT = 56 * 1024 * 1024
LOG2E = 1.4426950408889634


def _cparams(sem):
    return pltpu.CompilerParams(dimension_semantics=sem, vmem_limit_bytes=VMEM_LIMIT)


def _silu(x):
    return x * (1.0 / (1.0 + jnp.exp(-x)))


MOD_TN = 1024


def _mod_kernel(c_ref, w_ref, b_ref, o_ref):
    c = _silu(c_ref[...]).astype(BF16)
    o_ref[...] = jnp.dot(c, w_ref[...].astype(BF16), preferred_element_type=F32) + b_ref[...]


def _modulation(c_rows, ada_w, ada_b):
    n_out = N_MOD * D_MODEL
    return pl.pallas_call(
        _mod_kernel,
        grid=(DEPTH, n_out // MOD_TN),
        in_specs=[pl.BlockSpec((8, D_MODEL), lambda l, j: (0, 0)),
                  pl.BlockSpec((None, D_MODEL, MOD_TN), lambda l, j: (l, 0, j)),
                  pl.BlockSpec((None, 1, MOD_TN), lambda l, j: (l, 0, j))],
        out_specs=pl.BlockSpec((None, 8, MOD_TN), lambda l, j: (l, 0, j)),
        out_shape=jax.ShapeDtypeStruct((DEPTH, 8, n_out), F32),
        compiler_params=_cparams(("parallel", "parallel")),
        name="adaln_mod",
    )(c_rows, ada_w, ada_b)


PROJ_TN = 512
N_PROJ_TILES = D_MAIN // PROJ_TN


def _ada_norm_rows(x, gain, shift, scale):
    y = x * lax.rsqrt(jnp.mean(x * x, axis=-1, keepdims=True) + NORM_EPS)
    return (y * gain) * (1.0 + scale) + shift


def _head_norm(a, gain):
    return a * lax.rsqrt(jnp.mean(a * a, axis=-1, keepdims=True) + NORM_EPS) * gain


def _rope(a, cos, sin):
    lane = lax.broadcasted_iota(jnp.int32, a.shape, 1)
    partner = jnp.where((lane % 64) < 32, pltpu.roll(a, LANES - 32, 1), pltpu.roll(a, 32, 1))
    return a * cos + partner * sin


def _proj_kernel(x_ref, mod_ref, g_ref, w_ref, ws_ref, gains_ref, gp_ref, cos_ref, sin_ref,
                 o_ref, gates_ref, h_ref, *, rope, tm):
    j = pl.program_id(2)

    @pl.when(j == 0)
    def _():
        h = _ada_norm_rows(x_ref[...], g_ref[...], mod_ref[0:1, :], mod_ref[1:2, :])
        hb = h.astype(BF16)
        h_ref[...] = hb
        raw = jnp.dot(hb, ws_ref[...], preferred_element_type=F32)
        lane = lax.broadcasted_iota(jnp.int32, raw.shape, 1)
        beta = 1.0 / (1.0 + jnp.exp(-raw))
        z = raw + gp_ref[1:2, :]
        softplus = jnp.maximum(z, 0.0) + jnp.log1p(jnp.exp(-jnp.abs(z)))
        gates_ref[...] = jnp.where(lane < 2 * DN_HEADS, beta, -jnp.exp(gp_ref[0:1, :]) * softplus)

    acc = jnp.dot(h_ref[...], w_ref[...], preferred_element_type=F32)

    def normed(col, gi, with_rope):
        a = _head_norm(acc[:, col * LANES:(col + 1) * LANES], gains_ref[gi:gi + 1, :])
        if with_rope and rope:
            a = _rope(a, cos_ref[...], sin_ref[...])
        return a.astype(BF16)

    @pl.when(j < 2)
    def _():
        gain = gains_ref[pl.ds(j, 1), :]
        for col in range(4):
            o_ref[:, col * LANES:(col + 1) * LANES] = _head_norm(
                acc[:, col * LANES:(col + 1) * LANES], gain).astype(BF16)

    @pl.when(jnp.logical_or(j == 3, j == 4))
    def _():
        for col in range(4):
            o_ref[:, col * LANES:(col + 1) * LANES] = normed(col, 2, True)

    @pl.when(j == 5)
    def _():
        for col in range(2):
            o_ref[:, col * LANES:(col + 1) * LANES] = normed(col, 3, True)
        o_ref[:, 2 * LANES:] = acc[:, 2 * LANES:].astype(BF16)

    @pl.when(jnp.logical_or(j == 2, j > 5))
    def _():
        o_ref[...] = acc.astype(BF16)


def _project(x, mod, g_mix, w_main, w_small, gains, gate_params, cos_t, sin_t, *, rope, tm, per_batch_mod):
    B, N, _ = x.shape
    mod_idx = (lambda b, i, j: (b, 0, 0)) if per_batch_mod else (lambda b, i, j: (0, 0, 0))
    return pl.pallas_call(
        functools.partial(_proj_kernel, rope=rope, tm=tm),
        grid=(B, N // tm, N_PROJ_TILES),
        in_specs=[pl.BlockSpec((None, tm, D_MODEL), lambda b, i, j: (b, i, 0)),
                  pl.BlockSpec((None, 8, D_MODEL), mod_idx),
                  pl.BlockSpec((1, D_MODEL), lambda b, i, j: (0, 0)),
                  pl.BlockSpec((D_MODEL, PROJ_TN), lambda b, i, j: (0, j)),
                  pl.BlockSpec((D_MODEL, LANES), lambda b, i, j: (0, 0)),
                  pl.BlockSpec((8, LANES), lambda b, i, j: (0, 0)),
                  pl.BlockSpec((8, LANES), lambda b, i, j: (0, 0)),
                  pl.BlockSpec((tm, LANES), lambda b, i, j: (i, 0)),
                  pl.BlockSpec((tm, LANES), lambda b, i, j: (i, 0))],
        out_specs=[pl.BlockSpec((None, tm, PROJ_TN), lambda b, i, j: (b, i, j)),
                   pl.BlockSpec((None, tm, LANES), lambda b, i, j: (b, i, 0))],
        out_shape=[jax.ShapeDtypeStruct((B, N, D_MAIN), BF16),
                   jax.ShapeDtypeStruct((B, N, LANES), F32)],
        scratch_shapes=[pltpu.VMEM((tm, D_MODEL), BF16)],
        compiler_params=_cparams(("parallel", "parallel", "arbitrary")),
        name="in_proj",
    )(x, mod, g_mix, w_main, w_small, gains, gate_params, cos_t, sin_t)


SM_SCALE = HEAD_DIM ** -0.5
_NT = (((1,), (1,)), ((), ()))
COL_QA, COL_KA, COL_VA = 0, W_A // LANES, 2 * W_A // LANES
COL_QB = 3 * W_A // LANES
COL_KB = COL_QB + W_B // LANES
COL_VB = COL_KB + W_KV // LANES
COL_QC = COL_VB + W_KV // LANES
COL_GC = COL_QC + 3 * W_C // LANES


def _softmax_pv(score_blocks, value_blocks):
    m = functools.reduce(jnp.maximum, [jnp.max(s, axis=-1, keepdims=True) for s in score_blocks])
    ps = [jnp.exp2((s - m) * (SM_SCALE * LOG2E)) for s in score_blocks]
    l = functools.reduce(jnp.add, [jnp.sum(p, axis=-1, keepdims=True) for p in ps])
    o = functools.reduce(jnp.add, [jnp.dot(p.astype(BF16), v, preferred_element_type=F32)
                                   for p, v in zip(ps, value_blocks)])
    return o / l


GQA_TQ = 512
GQA_GROUP = GQA_Q_HEADS // GQA_KV_HEADS


def _gqa_kernel(q_ref, kl_ref, vl_ref, kc_ref, vc_ref, o_ref):
    for r in range(GQA_GROUP):
        q = q_ref[:, r * LANES:(r + 1) * LANES]
        s_c = lax.dot_general(q, kc_ref[...], _NT, preferred_element_type=F32)
        s_l = lax.dot_general(q, kl_ref[...], _NT, preferred_element_type=F32)
        o = _softmax_pv([s_c, s_l], [vc_ref[...], vl_ref[...]])
        o_ref[:, r * LANES:(r + 1) * LANES] = o.astype(BF16)


def _gqa_attention(p_lat, p_ctx):
    B, N, _ = p_lat.shape
    M = p_ctx.shape[1]
    qw = GQA_GROUP * LANES
    return pl.pallas_call(
        _gqa_kernel,
        grid=(B, GQA_KV_HEADS, N // GQA_TQ),
        in_specs=[pl.BlockSpec((None, GQA_TQ, qw), lambda b, g, i: (b, i, COL_QB // GQA_GROUP + g)),
                  pl.BlockSpec((None, N, LANES), lambda b, g, i: (b, 0, COL_KB + g)),
                  pl.BlockSpec((None, N, LANES), lambda b, g, i: (b, 0, COL_VB + g)),
                  pl.BlockSpec((None, M, LANES), lambda b, g, i: (b, 0, COL_KB + g)),
                  pl.BlockSpec((None, M, LANES), lambda b, g, i: (b, 0, COL_VB + g))],
        out_specs=pl.BlockSpec((None, GQA_TQ, qw), lambda b, g, i: (b, i, g)),
        out_shape=jax.ShapeDtypeStruct((B, N, W_B), BF16),
        compiler_params=_cparams(("parallel", "parallel", "parallel")),
        name="gqa_attention",
    )(p_lat, p_lat, p_lat, p_ctx, p_ctx)


def _ctx_attn_kernel(p_ref, oa_ref, ob_ref):
    def col(c):
        return p_ref[:, c * LANES:(c + 1) * LANES]
    for h in range(NA_HEADS):
        s = lax.dot_general(col(COL_QA + h), col(COL_KA + h), _NT, preferred_element_type=F32)
        oa_ref[:, h * LANES:(h + 1) * LANES] = _softmax_pv([s], [col(COL_VA + h)]).astype(BF16)
    for h in range(GQA_Q_HEADS):
        g = h // GQA_GROUP
        s = lax.dot_general(col(COL_QB + h), col(COL_KB + g), _NT, preferred_element_type=F32)
        ob_ref[:, h * LANES:(h + 1) * LANES] = _softmax_pv([s], [col(COL_VB + g)]).astype(BF16)


def _ctx_attention(p_ctx):
    B, M, _ = p_ctx.shape
    return pl.pallas_call(
        _ctx_attn_kernel,
        grid=(B,),
        in_specs=[pl.BlockSpec((None, M, D_MAIN), lambda b: (b, 0, 0))],
        out_specs=[pl.BlockSpec((None, M, W_A), lambda b: (b, 0, 0)),
                   pl.BlockSpec((None, M, W_B), lambda b: (b, 0, 0))],
        out_shape=[jax.ShapeDtypeStruct((B, M, W_A), BF16), jax.ShapeDtypeStruct((B, M, W_B), BF16)],
        compiler_params=_cparams(("parallel",)),
        name="ctx_attention",
    )(p_ctx)


NA_ROWS = 8
NA_KROWS = 16
NA_TQ = NA_ROWS * GRID_W
NA_TK = NA_KROWS * GRID_W


def _na_kernel(q_ref, k_ref, v_ref, kc_ref, vc_ref, bias_ref, o_ref, *, n_rows):
    g = pl.program_id(2)
    kbase = jnp.clip(g * NA_ROWS - NA_WIN_R // 2, 0, n_rows - NA_KROWS)
    off = pl.multiple_of(kbase * GRID_W, GRID_W)
    q = q_ref[...]
    k = k_ref[pl.ds(off, NA_TK), :]
    v = v_ref[pl.ds(off, NA_TK), :]
    s_loc = lax.dot_general(q, k, _NT, preferred_element_type=F32) + bias_ref[...]
    s_ctx = lax.dot_general(q, kc_ref[...], _NT, preferred_element_type=F32)
    o_ref[...] = _softmax_pv([s_loc, s_ctx], [v, vc_ref[...]]).astype(BF16)


def _na_bias_tables(rpb, n_rows):
    wr = min(NA_WIN_R, n_rows)
    tabs = []
    for g in (0, 1, n_rows // NA_ROWS - 1):
        r0 = g * NA_ROWS
        kbase = min(max(r0 - NA_WIN_R // 2, 0), n_rows - NA_KROWS)
        qr = r0 + jnp.arange(NA_ROWS)
        kr = kbase + jnp.arange(NA_KROWS)
        rstart = jnp.clip(qr - wr // 2, 0, n_rows - wr)
        in_row = (kr[None, :] >= rstart[:, None]) & (kr[None, :] < rstart[:, None] + wr)
        dr = jnp.clip(kr[None, :] - qr[:, None] + (NA_WIN_R - 1), 0, 2 * NA_WIN_R - 2)
        qc = jnp.arange(GRID_W)
        kc = jnp.arange(GRID_W)
        cstart = jnp.clip(qc - NA_WIN_C // 2, 0, GRID_W - NA_WIN_C)
        in_col = (kc[None, :] >= cstart[:, None]) & (kc[None, :] < cstart[:, None] + NA_WIN_C)
        dc = jnp.clip(kc[None, :] - qc[:, None] + (NA_WIN_C - 1), 0, 2 * NA_WIN_C - 2)
        bias = rpb[:, dr[:, None, :, None], dc[None, :, None, :]]
        ok = in_row[:, None, :, None] & in_col[None, :, None, :]
        tab = jnp.where(ok[None], bias.astype(F32) / SM_SCALE, NEG_INF)
        tabs.append(tab.reshape(NA_HEADS, NA_TQ, NA_TK))
    return jnp.stack(tabs, axis=1)


def _na_attention(p_lat, p_ctx, bias_tabs):
    B, N, _ = p_lat.shape
    M = p_ctx.shape[1]
    n_rows = N // GRID_W
    n_groups = n_rows // NA_ROWS
    variant = lambda g: jnp.where(g == 0, 0, jnp.where(g == n_groups - 1, 2, 1))
    return pl.pallas_call(
        functools.partial(_na_kernel, n_rows=n_rows),
        grid=(B, NA_HEADS, n_groups),
        in_specs=[pl.BlockSpec((None, NA_TQ, LANES), lambda b, h, g: (b, g, COL_QA + h)),
                  pl.BlockSpec((None, N, LANES), lambda b, h, g: (b, 0, COL_KA + h)),
                  pl.BlockSpec((None, N, LANES), lambda b, h, g: (b, 0, COL_VA + h)),
                  pl.BlockSpec((None, M, LANES), lambda b, h, g: (b, 0, COL_KA + h)),
                  pl.BlockSpec((None, M, LANES), lambda b, h, g: (b, 0, COL_VA + h)),
                  pl.BlockSpec((None, None, NA_TQ, NA_TK), lambda b, h, g: (h, variant(g), 0, 0))],
        out_specs=pl.BlockSpec((None, NA_TQ, LANES), lambda b, h, g: (b, g, h)),
        out_shape=jax.ShapeDtypeStruct((B, N, W_A), BF16),
        compiler_params=_cparams(("parallel", "parallel", "parallel")),
        name="na_attention",
    )(p_lat, p_lat, p_lat, p_ctx, p_ctx, bias_tabs)


DN_L = 128
DN_HALF = DN_L // 2
DN_CHAINS = 2 * DN_HEADS
_TN = (((0,), (0,)), ((), ()))


def _dn_conv_kernel(xc_ref, xl_ref, w_ref, o_ref, pad_ref, *, n_ctx, n_lat):
    which = pl.program_id(1)
    half = DN_CONV // 2
    q_scale = jnp.where(which == 0, HEAD_DIM ** -0.5, 1.0)

    def run(x_ref, n, out_off):
        pad_ref[0:8, :] = jnp.zeros((8, LANES), F32)
        pad_ref[8 + n:16 + n, :] = jnp.zeros((8, LANES), F32)
        pad_ref[8:8 + n, :] = x_ref[...].astype(F32)
        chunk = min(n, 256)
        for c0 in range(0, n, chunk):
            y = jnp.zeros((chunk, LANES), F32)
            for i in range(DN_CONV):
                y = y + w_ref[i:i + 1, :] * pad_ref[8 + c0 + i - half:8 + c0 + i - half + chunk, :]
            y = _silu(y)
            yn = y * lax.rsqrt(jnp.sum(y * y, axis=-1, keepdims=True) + NORM_EPS) * q_scale
            o_ref[out_off + c0:out_off + c0 + chunk, :] = jnp.where(which < 2, yn, y)

    run(xc_ref, n_ctx, 0)
    run(xl_ref, n_lat, n_ctx)


def _dn_conv(p_ctx, p_lat, conv_w):
    B, N, _ = p_lat.shape
    M = p_ctx.shape[1]
    col = lambda b, s, h: (b, 0, COL_QC + s * DN_HEADS + h)
    return pl.pallas_call(
        functools.partial(_dn_conv_kernel, n_ctx=M, n_lat=N),
        grid=(B, 3, DN_HEADS),
        in_specs=[pl.BlockSpec((None, M, LANES), col),
                  pl.BlockSpec((None, N, LANES), col),
                  pl.BlockSpec((8, LANES), lambda b, s, h: (0, s * DN_HEADS + h))],
        out_specs=pl.BlockSpec((None, None, M + N, LANES), lambda b, s, h: (b, s, 0, h)),
        out_shape=jax.ShapeDtypeStruct((B, 3, M + N, W_C), F32),
        scratch_shapes=[pltpu.VMEM((N + 16, LANES), F32)],
        compiler_params=_cparams(("parallel", "parallel", "parallel")),
        name="dn_conv",
    )(p_ctx, p_lat, conv_w)


def _scan_incl(x, axis):
    idx = lax.broadcasted_iota(jnp.int32, x.shape, axis)
    s = 1
    while s < DN_L:
        x = x + jnp.where(idx >= s, pltpu.roll(x, s, axis), 0.0)
        s *= 2
    return x


def _dn_gate_sums(gates, gates_t):
    pf = _scan_incl(gates, 0)
    tot = pf[DN_L - 1:DN_L, :]
    pf_t = _scan_incl(gates_t, 1)
    tot_t = pf_t[:, DN_L - 1:DN_L]
    return (pf, tot - pf + gates), (pf_t, tot_t - pf_t + gates_t), tot


def _dn_chain_terms(h, d, gates, gates_t, sums):
    (cum, cum_t, tot) = sums
    lb = d * DN_HEADS + h
    lg = 2 * DN_HEADS + lb
    row = lax.broadcasted_iota(jnp.int32, (DN_L, DN_L), 0)
    col = lax.broadcasted_iota(jnp.int32, (DN_L, DN_L), 1)
    return (gates[:, lb:lb + 1], gates_t[lb:lb + 1, :], cum[d][:, lg:lg + 1], cum_t[d][lg:lg + 1, :],
            tot[:, lg:lg + 1], row, col)


def _dn_lower_form(h, d, gram, gates, gates_t, sums):
    beta_c, beta_r, g_c, g_r, _, row, col = _dn_chain_terms(h, d, gates, gates_t, sums)
    if d == 0:
        m = beta_c * gram * jnp.exp(jnp.where(row > col, g_c - g_r, NEG_INF))
    else:
        m = beta_r * gram * jnp.exp(jnp.where(row > col, g_r - g_c, NEG_INF))
    return jnp.where(row > col, m, 0.0)


def _dn_gram_kernel(k_ref, g_ref, gt_ref, m_ref):
    gates, gates_t = g_ref[...], gt_ref[...]
    sums = _dn_gate_sums(gates, gates_t)
    for h in range(DN_HEADS):
        kb = k_ref[:, h * LANES:(h + 1) * LANES].astype(BF16)
        gram = lax.dot_general(kb, kb, _NT, preferred_element_type=F32)
        for d in range(2):
            m_ref[d * DN_HEADS + h] = _dn_lower_form(h, d, gram, gates, gates_t, sums)


def _dn_gram(dnx, gates, gates_t):
    B, _, T, _ = dnx.shape
    n_sc = T // DN_L
    return pl.pallas_call(
        _dn_gram_kernel,
        grid=(B, n_sc),
        in_specs=[pl.BlockSpec((None, None, DN_L, W_C), lambda b, s: (b, 1, s, 0)),
                  pl.BlockSpec((None, DN_L, LANES), lambda b, s: (b, s, 0)),
                  pl.BlockSpec((None, 16, DN_L), lambda b, s: (b, 0, s))],
        out_specs=pl.BlockSpec((None, None, DN_CHAINS, DN_L, DN_L), lambda b, s: (b, s, 0, 0, 0)),
        out_shape=jax.ShapeDtypeStruct((B, n_sc, DN_CHAINS, DN_L, DN_L), F32),
        compiler_params=_cparams(("parallel", "parallel")),
        name="dn_gram",
    )(dnx, gates, gates_t)


def _tri_inv_kernel(a_ref, t_ref):
    n = DN_HALF
    t_ref[...] = jnp.zeros(t_ref.shape, F32)
    col_id = lax.broadcasted_iota(jnp.int32, (n, LANES), 0)

    def row_step(i, carry):
        def blk(jb, acc):
            for jj in range(8):
                j = jb * 8 + jj
                a = a_ref[pl.ds(i * n + j, 1), :]
                acc = acc - a * t_ref[pl.ds(pl.multiple_of(j * n, n), n), :]
            return acc
        acc = lax.fori_loop(0, (i + 7) // 8, blk, jnp.where(col_id == i, 1.0, 0.0))
        t_ref[pl.ds(pl.multiple_of(i * n, n), n), :] = acc
        return carry

    lax.fori_loop(0, n, row_step, 0)


def _tri_inv(a_t):
    return pl.pallas_call(
        _tri_inv_kernel,
        grid=(a_t.shape[1] // LANES,),
        in_specs=[pl.BlockSpec((DN_HALF * DN_HALF, LANES), lambda g: (0, g))],
        out_specs=pl.BlockSpec((DN_HALF * DN_HALF, LANES), lambda g: (0, g)),
        out_shape=jax.ShapeDtypeStruct(a_t.shape, F32),
        compiler_params=_cparams(("parallel",)),
        name="dn_tri_inv",
    )(a_t)


def _dn_prep_kernel(q_ref, k_ref, v_ref, g_ref, gt_ref, m_ref, t_ref,
                    u_ref, w_ref, kt_ref, qh_ref, qk_ref, gl_ref):
    gates, gates_t = g_ref[...], gt_ref[...]
    sums = _dn_gate_sums(gates, gates_t)
    H = DN_HALF
    for h in range(DN_HEADS):
        q = q_ref[:, h * LANES:(h + 1) * LANES]
        k = k_ref[:, h * LANES:(h + 1) * LANES]
        v = v_ref[:, h * LANES:(h + 1) * LANES]
        qk_raw = lax.dot_general(q.astype(BF16), k.astype(BF16), _NT, preferred_element_type=F32)
        for d in range(2):
            c = d * DN_HEADS + h
            beta_c, _, g_c, g_r, g_tot, row, col = _dn_chain_terms(h, d, gates, gates_t, sums)
            causal = (row >= col) if d == 0 else (row <= col)
            decay = jnp.exp(jnp.where(causal, g_c - g_r, NEG_INF))
            eg = jnp.exp(g_c)
            rhs = jnp.concatenate([v * beta_c, k * (beta_c * eg)], axis=1)
            t1 = t_ref[c, 0].astype(BF16)
            t2 = t_ref[c, 1].astype(BF16)
            m21 = m_ref[c, H:, :H].astype(BF16)
            if d == 0:
                top = jnp.dot(t1, rhs[:H].astype(BF16), preferred_element_type=F32)
                res = rhs[H:] - jnp.dot(m21, top.astype(BF16), preferred_element_type=F32)
                bot = jnp.dot(t2, res.astype(BF16), preferred_element_type=F32)
            else:
                bot = lax.dot_general(t2, rhs[H:].astype(BF16), _TN, preferred_element_type=F32)
                res = rhs[:H] - lax.dot_general(m21, bot.astype(BF16), _TN, preferred_element_type=F32)
                top = lax.dot_general(t1, res.astype(BF16), _TN, preferred_element_type=F32)
            sol = jnp.concatenate([top, bot], axis=0)
            u_ref[c] = sol[:, :LANES]
            w_ref[c] = sol[:, LANES:].astype(BF16)
            kt_ref[c] = (k * jnp.exp(g_tot - g_c)).T.astype(BF16)
            qh_ref[c] = (q * eg).astype(BF16)
            qk_ref[c] = (qk_raw * decay).astype(BF16)
            gl_ref[c] = jnp.broadcast_to(jnp.exp(g_tot), (8, LANES))


def _dn_prep(dnx, gates, gates_t, m_low, t_inv):
    B, _, T, _ = dnx.shape
    n_sc = T // DN_L
    big = lambda dt: jax.ShapeDtypeStruct((B, n_sc, DN_CHAINS, DN_L, DN_L), dt)
    blk = pl.BlockSpec((None, None, DN_CHAINS, DN_L, DN_L), lambda b, s: (b, s, 0, 0, 0))
    return pl.pallas_call(
        _dn_prep_kernel,
        grid=(B, n_sc),
        in_specs=[pl.BlockSpec((None, None, DN_L, W_C), lambda b, s: (b, 0, s, 0)),
                  pl.BlockSpec((None, None, DN_L, W_C), lambda b, s: (b, 1, s, 0)),
                  pl.BlockSpec((None, None, DN_L, W_C), lambda b, s: (b, 2, s, 0)),
                  pl.BlockSpec((None, DN_L, LANES), lambda b, s: (b, s, 0)),
                  pl.BlockSpec((None, 16, DN_L), lambda b, s: (b, 0, s)),
                  blk,
                  pl.BlockSpec((None, None, DN_CHAINS, 2, DN_HALF, DN_HALF), lambda b, s: (b, s, 0, 0, 0, 0))],
        out_specs=[blk, blk, blk, blk, blk,
                   pl.BlockSpec((None, None, DN_CHAINS, 8, LANES), lambda b, s: (b, s, 0, 0, 0))],
        out_shape=[big(F32), big(BF16), big(BF16), big(BF16), big(BF16),
                   jax.ShapeDtypeStruct((B, n_sc, DN_CHAINS, 8, LANES), F32)],
        compiler_params=_cparams(("parallel", "parallel")),
        name="dn_prep",
    )(dnx, dnx, dnx, gates, gates_t, m_low, t_inv)


def _dn_scan_kernel(uf, wf, ktf, qhf, qkf, glf, ub, wb, ktb, qhb, qkb, glb, of_ref, ob_ref, s_ref):
    @pl.when(pl.program_id(1) == 0)
    def _():
        s_ref[...] = jnp.zeros(s_ref.shape, F32)

    for d, (u, w, kt, qh, qk, gl, o_ref) in enumerate(((uf, wf, ktf, qhf, qkf, glf, of_ref),
                                                       (ub, wb, ktb, qhb, qkb, glb, ob_ref))):
        for h in range(DN_HEADS):
            c = d * DN_HEADS + h
            s = s_ref[c]
            sb = s.astype(BF16)
            v_new = u[h] - jnp.dot(w[h], sb, preferred_element_type=F32)
            vb = v_new.astype(BF16)
            o = jnp.dot(qh[h], sb, preferred_element_type=F32) + jnp.dot(qk[h], vb, preferred_element_type=F32)
            s_ref[c] = s * gl[h][0:1, :] + jnp.dot(kt[h], vb, preferred_element_type=F32)
            o_ref[:, h * LANES:(h + 1) * LANES] = o


def _dn_scan(u, w, kt, qh, qk, gl, n_ctx_chunks):
    B, n_sc = u.shape[0], u.shape[1]
    fwd = lambda b, s: s
    bwd = lambda b, s: jnp.where(s < n_ctx_chunks, n_ctx_chunks - 1 - s, n_sc - 1 + n_ctx_chunks - s)
    specs = []
    for d, walk in ((0, fwd), (1, bwd)):
        for arr in (u, w, kt, qh, qk):
            specs.append(pl.BlockSpec((None, None, DN_HEADS, DN_L, DN_L),
                                      functools.partial(lambda b, s, d, walk: (b, walk(b, s), d, 0, 0), d=d, walk=walk)))
        specs.append(pl.BlockSpec((None, None, DN_HEADS, 8, LANES),
                                  functools.partial(lambda b, s, d, walk: (b, walk(b, s), d, 0, 0), d=d, walk=walk)))
    out = jax.ShapeDtypeStruct((B, n_sc * DN_L, W_C), F32)
    return pl.pallas_call(
        _dn_scan_kernel,
        grid=(B, n_sc),
        in_specs=specs,
        out_specs=[pl.BlockSpec((None, DN_L, W_C), lambda b, s: (b, fwd(b, s), 0)),
                   pl.BlockSpec((None, DN_L, W_C), lambda b, s: (b, bwd(b, s), 0))],
        out_shape=[out, out],
        scratch_shapes=[pltpu.VMEM((DN_CHAINS, DN_L, DN_L), F32)],
        compiler_params=_cparams(("parallel", "arbitrary")),
        name="dn_scan",
    )(u, w, kt, qh, qk, gl, u, w, kt, qh, qk, gl)


def _deltanet(p_ctx, p_lat, gates_ctx, gates_lat, conv_w):
    B = p_lat.shape[0]
    n_ctx_chunks = p_ctx.shape[1] // DN_L
    dnx = _dn_conv(p_ctx, p_lat, conv_w)
    gates = jnp.concatenate([gates_ctx, gates_lat], axis=1)
    gates_t = jnp.swapaxes(gates[:, :, :16], 1, 2)
    m_low = _dn_gram(dnx, gates, gates_t)
    n_sc = m_low.shape[1]
    H = DN_HALF
    mm = m_low.reshape(B * n_sc * DN_CHAINS, 2, H, 2, H)
    diag = jnp.stack([mm[:, 0, :, 0, :], mm[:, 1, :, 1, :]], axis=1)
    a_t = jnp.transpose(diag, (2, 3, 0, 1)).reshape(H * H, -1)
    n_sys = a_t.shape[1]
    a_t = jnp.pad(a_t, ((0, 0), (0, -n_sys % LANES)))
    t_t = _tri_inv(a_t)[:, :n_sys]
    t_inv = jnp.transpose(t_t.reshape(H, H, -1, 2), (2, 3, 0, 1)).reshape(B, n_sc, DN_CHAINS, 2, H, H)
    u, w, kt, qh, qk, gl = _dn_prep(dnx, gates, gates_t, m_low, t_inv)
    return _dn_scan(u, w, kt, qh, qk, gl, n_ctx_chunks)


OUT_TM = 256


def _out_proj_kernel(oa_ref, ob_ref, of_ref, obk_ref, gc_ref, x_ref, mod_ref, gain_ref, w_ref, o_ref):
    o_dn = of_ref[...] + obk_ref[...]
    acc = jnp.dot(oa_ref[...], w_ref[0:W_A, :], preferred_element_type=F32)
    acc = acc + jnp.dot(ob_ref[...], w_ref[W_A:W_A + W_B, :], preferred_element_type=F32)
    for h in range(DN_HEADS):
        oh = _head_norm(o_dn[:, h * LANES:(h + 1) * LANES], gain_ref[0:1, :])
        gate = gc_ref[:, h * LANES:(h + 1) * LANES].astype(F32)
        oc = (oh * _silu(gate)).astype(BF16)
        lo = W_A + W_B + h * LANES
        acc = acc + jnp.dot(oc, w_ref[lo:lo + LANES, :], preferred_element_type=F32)
    o_ref[...] = x_ref[...] + mod_ref[2:3, :] * acc


def _out_project(oa, ob, o_f, o_b, proj, x, mod, out_gain, w_out, *, dn_row_off, per_batch_mod):
    B, N, _ = x.shape
    tm = OUT_TM
    off = dn_row_off // tm
    mod_idx = (lambda b, i: (b, 0, 0)) if per_batch_mod else (lambda b, i: (0, 0, 0))
    return pl.pallas_call(
        _out_proj_kernel,
        grid=(B, N // tm),
        in_specs=[pl.BlockSpec((None, tm, W_A), lambda b, i: (b, i, 0)),
                  pl.BlockSpec((None, tm, W_B), lambda b, i: (b, i, 0)),
                  pl.BlockSpec((None, tm, W_C), lambda b, i: (b, i + off, 0)),
                  pl.BlockSpec((None, tm, W_C), lambda b, i: (b, i + off, 0)),
                  pl.BlockSpec((None, tm, W_C), lambda b, i: (b, i, COL_GC * LANES // W_C)),
                  pl.BlockSpec((None, tm, D_MODEL), lambda b, i: (b, i, 0)),
                  pl.BlockSpec((None, 8, D_MODEL), mod_idx),
                  pl.BlockSpec((8, LANES), lambda b, i: (0, 0)),
                  pl.BlockSpec((MIX_WIDTH, D_MODEL), lambda b, i: (0, 0))],
        out_specs=pl.BlockSpec((None, tm, D_MODEL), lambda b, i: (b, i, 0)),
        out_shape=jax.ShapeDtypeStruct(x.shape, F32),
        compiler_params=_cparams(("parallel", "parallel")),
        name="out_proj",
    )(oa, ob, o_f, o_b, proj, x, mod, out_gain, w_out)


ROUTER_TM = 512


def _router_kernel(x_ref, mod_ref, g_ref, rw_ref, aff_ref):
    h = _ada_norm_rows(x_ref[...], g_ref[...], mod_ref[3:4, :], mod_ref[4:5, :])
    logits = jnp.dot(h, rw_ref[...], preferred_element_type=F32, precision=lax.Precision.HIGHEST)
    lane = lax.broadcasted_iota(jnp.int32, logits.shape, 1)
    logits = jnp.where(lane < N_EXPERTS, logits, NEG_INF)
    p = jnp.exp(logits - jnp.max(logits, axis=-1, keepdims=True))
    aff_ref[...] = p / jnp.sum(p, axis=-1, keepdims=True)


def _router(x, mod, g_ffn, router_w, *, per_batch_mod):
    B, N, _ = x.shape
    tm = min(ROUTER_TM, N)
    mod_idx = (lambda b, i: (b, 0, 0)) if per_batch_mod else (lambda b, i: (0, 0, 0))
    return pl.pallas_call(
        _router_kernel,
        grid=(B, N // tm),
        in_specs=[pl.BlockSpec((None, tm, D_MODEL), lambda b, i: (b, i, 0)),
                  pl.BlockSpec((None, 8, D_MODEL), mod_idx),
                  pl.BlockSpec((1, D_MODEL), lambda b, i: (0, 0)),
                  pl.BlockSpec((D_MODEL, LANES), lambda b, i: (0, 0))],
        out_specs=pl.BlockSpec((None, tm, LANES), lambda b, i: (b, i, 0)),
        out_shape=jax.ShapeDtypeStruct((B, N, LANES), F32),
        compiler_params=_cparams(("parallel", "parallel")),
        name="moe_router",
    )(x, mod, g_ffn, router_w)


def _select_kernel(aff_ref, idx_ref, gate_ref, afft_ref, slot_ref, *, n_tok, cap):
    n_blk = n_tok // LANES
    for t in range(n_blk):
        afft_ref[:, t * LANES:(t + 1) * LANES] = aff_ref[t * LANES:(t + 1) * LANES, :].T[:N_EXPERTS, :]
    afft = afft_ref[...]
    bits = pltpu.bitcast(afft, jnp.int32)
    count = lambda m: jnp.sum(jnp.where(m, 1.0, 0.0), axis=1, keepdims=True)

    def thr_step(i, t):
        cand = t | (jnp.int32(1) << (30 - i))
        return jnp.where(count(bits >= cand) >= cap, cand, t)
    thr = lax.fori_loop(0, 31, thr_step, jnp.zeros((N_EXPERTS, 1), jnp.int32))
    above = bits > thr
    tie = bits == thr
    need = cap - count(above)
    tok = lax.broadcasted_iota(jnp.int32, (N_EXPERTS, n_tok), 1)
    n_bits = n_tok.bit_length()

    def cut_step(i, cut):
        cand = cut + (jnp.int32(1) << (n_bits - 1 - i))
        ok = jnp.logical_and(cand <= n_tok, count(jnp.logical_and(tie, tok < cand)) <= need)
        return jnp.where(ok, cand, cut)
    cut = lax.fori_loop(0, n_bits, cut_step, jnp.zeros((N_EXPERTS, 1), jnp.int32))
    sel = jnp.logical_or(above, jnp.logical_and(tie, tok < cut))

    r_i = lax.broadcasted_iota(jnp.int32, (LANES, LANES), 0)
    c_i = lax.broadcasted_iota(jnp.int32, (LANES, LANES), 1)
    tri = jnp.where(r_i <= c_i, 1.0, 0.0).astype(BF16)
    base = jnp.zeros((N_EXPERTS, 1), F32)
    for t in range(n_blk):
        s_blk = sel[:, t * LANES:(t + 1) * LANES]
        ones = jnp.where(s_blk, 1.0, 0.0)
        incl = jnp.dot(ones.astype(BF16), tri, preferred_element_type=F32)
        slot_ref[:, t * LANES:(t + 1) * LANES] = jnp.where(s_blk, base + incl - 1.0, -1.0)
        base = base + incl[:, LANES - 1:LANES]

    tok_f = lax.broadcasted_iota(jnp.int32, (8, n_tok), 1).astype(F32)
    sub = lax.broadcasted_iota(jnp.int32, (8, n_tok), 0).astype(F32)
    lane_t = lax.broadcasted_iota(jnp.int32, (8, LANES), 1)

    def expert(e, carry):
        slot_e = slot_ref[pl.ds(e, 1), :]
        aff_e = afft_ref[pl.ds(e, 1), :]

        def group(sg, tiles):
            idx_t, gate_t = tiles
            match = slot_e == (sub + jnp.asarray(sg * 8, F32))
            idx_c = jnp.sum(jnp.where(match, tok_f, 0.0), axis=1, keepdims=True)
            gate_c = jnp.sum(jnp.where(match, aff_e, 0.0), axis=1, keepdims=True)
            here = lane_t == sg
            return jnp.where(here, idx_c, idx_t), jnp.where(here, gate_c, gate_t)

        zero = jnp.zeros((8, LANES), F32)
        idx_t, gate_t = lax.fori_loop(0, cap // 8, group, (zero, zero))
        idx_ref[e] = idx_t
        gate_ref[e] = gate_t
        return carry

    lax.fori_loop(0, N_EXPERTS, expert, 0)


def _select(aff, cap):
    S, n_tok, _ = aff.shape
    out = jax.ShapeDtypeStruct((S, N_EXPERTS, 8, LANES), F32)
    return pl.pallas_call(
        functools.partial(_select_kernel, n_tok=n_tok, cap=cap),
        grid=(S,),
        in_specs=[pl.BlockSpec((None, n_tok, LANES), lambda s: (s, 0, 0))],
        out_specs=[pl.BlockSpec((None, N_EXPERTS, 8, LANES), lambda s: (s, 0, 0, 0))] * 2,
        out_shape=[out, out],
        scratch_shapes=[pltpu.VMEM((N_EXPERTS, n_tok), F32), pltpu.VMEM((N_EXPERTS, n_tok), F32)],
        compiler_params=_cparams(("parallel",)),
        name="moe_select",
    )(aff)


FFN_TF = 256


def _moe_ffn_kernel(idx_ref, gate_ref, mod_ref, g_ref, wg_ref, wu_ref, wd_ref, x_hbm, acc_in, acc_hbm,
                    stage, hbuf, ybuf, sem, *, rows, n_f):
    del acc_in
    f = pl.program_id(1)
    s = pl.program_id(2)

    def row_copies(src, dst, scatter):
        def issue(r, c):
            t = idx_ref[0, r]
            if scatter:
                pltpu.make_async_copy(src.at[pl.ds(r, 1), :], dst.at[pl.ds(t, 1), :], sem).start()
            else:
                pltpu.make_async_copy(src.at[pl.ds(t, 1), :], dst.at[pl.ds(r, 1), :], sem).start()
            return c
        lax.fori_loop(0, rows, issue, 0)
        pltpu.make_async_copy(stage, stage, sem).wait()

    @pl.when(f == 0)
    def _():
        row_copies(x_hbm, stage, False)
        h = _ada_norm_rows(stage[...], g_ref[...], mod_ref[3:4, :], mod_ref[4:5, :])
        hbuf[s] = h.astype(BF16)
        ybuf[s] = jnp.zeros((rows, D_MODEL), F32)

    hb = hbuf[s]
    a = jnp.dot(hb, wg_ref[...].astype(BF16), preferred_element_type=F32)
    u = jnp.dot(hb, wu_ref[...].astype(BF16), preferred_element_type=F32)
    hid = (_silu(a) * u).astype(BF16)
    ybuf[s] = ybuf[s] + jnp.dot(hid, wd_ref[...].astype(BF16), preferred_element_type=F32)

    @pl.when(f == n_f - 1)
    def _():
        row_copies(acc_hbm, stage, False)
        stage[...] = stage[...] + (mod_ref[5:6, :] * gate_ref[...]) * ybuf[s]
        row_copies(stage, acc_hbm, True)


def _moe_ffn(idx, gate, x2d, mod, g_ffn, w_gate, w_up, w_down, layer, *, per_set_mod):
    E, S, _, rows = idx.shape
    n_f = D_EXPERT // FFN_TF
    mod_idx = (lambda e, f, s: (s, 0, 0)) if per_set_mod else (lambda e, f, s: (0, 0, 0))
    return pl.pallas_call(
        functools.partial(_moe_ffn_kernel, rows=rows, n_f=n_f),
        grid=(E, n_f, S),
        in_specs=[pl.BlockSpec((None, None, 1, rows), lambda e, f, s: (e, s, 0, 0), memory_space=pltpu.SMEM),
                  pl.BlockSpec((None, None, rows, 1), lambda e, f, s: (e, s, 0, 0)),
                  pl.BlockSpec((None, 8, D_MODEL), mod_idx),
                  pl.BlockSpec((1, D_MODEL), lambda e, f, s: (0, 0)),
                  pl.BlockSpec((None, None, D_MODEL, FFN_TF), lambda e, f, s: (layer, e, 0, f)),
                  pl.BlockSpec((None, None, D_MODEL, FFN_TF), lambda e, f, s: (layer, e, 0, f)),
                  pl.BlockSpec((None, None, FFN_TF, D_MODEL), lambda e, f, s: (layer, e, f, 0)),
                  pl.BlockSpec(memory_space=pl.ANY),
                  pl.BlockSpec(memory_space=pl.ANY)],
        out_specs=pl.BlockSpec(memory_space=pl.ANY),
        out_shape=jax.ShapeDtypeStruct(x2d.shape, F32),
        scratch_shapes=[pltpu.VMEM((rows, D_MODEL), F32),
                        pltpu.VMEM((S, rows, D_MODEL), BF16),
                        pltpu.VMEM((S, rows, D_MODEL), F32),
                        pltpu.SemaphoreType.DMA(())],
        input_output_aliases={8: 0},
        compiler_params=_cparams(("arbitrary", "arbitrary", "arbitrary")),
        name="moe_ffn",
    )(idx, gate, mod, g_ffn, w_gate, w_up, w_down, x2d, x2d)


def _tile_to_list(tile, cap):
    return jnp.swapaxes(tile[..., :cap // 8], -1, -2).reshape(tile.shape[0], N_EXPERTS, cap)


def _expert_choice(x, mod, g_ffn, router_w, w_gate, w_up, w_down, layer, *, per_batch_mod):
    B, N, _ = x.shape
    cap = (EC_FACTOR * N) // N_EXPERTS
    aff = _router(x, mod, g_ffn, router_w, per_batch_mod=per_batch_mod)
    idx_t, gate_t = _select(aff, cap)
    rows_of = jnp.arange(B, dtype=jnp.int32)[:, None, None] * N
    idx = _tile_to_list(idx_t, cap).astype(jnp.int32) + rows_of
    gate = _tile_to_list(gate_t, cap)
    if per_batch_mod:
        idx = jnp.swapaxes(idx, 0, 1)[:, :, None, :]
        gate = jnp.swapaxes(gate, 0, 1)[..., None]
    else:
        idx = jnp.swapaxes(idx, 0, 1).reshape(N_EXPERTS, 1, 1, B * cap)
        gate = jnp.swapaxes(gate, 0, 1).reshape(N_EXPERTS, 1, B * cap, 1)
    out = _moe_ffn(idx, gate, x.reshape(B * N, D_MODEL), mod, g_ffn, w_gate, w_up, w_down, layer,
                   per_set_mod=per_batch_mod)
    return out.reshape(B, N, D_MODEL)


def _rope_tables(n_tok):
    t = jnp.arange(n_tok, dtype=jnp.int32)
    pos = jnp.stack([t // GRID_W, t % GRID_W], axis=-1).astype(F32)
    n_freq = HEAD_DIM // 4
    freqs = ROPE_THETA ** (-jnp.arange(n_freq, dtype=F32) / n_freq)
    ang = pos[:, :, None] * freqs
    cos, sin = jnp.cos(ang), jnp.sin(ang)
    cos_t = jnp.concatenate([cos[:, 0], cos[:, 0], cos[:, 1], cos[:, 1]], axis=-1)
    sin_t = jnp.concatenate([-sin[:, 0], sin[:, 0], -sin[:, 1], sin[:, 1]], axis=-1)
    return cos_t, sin_t


def _pad_rows(a, rows=8):
    return jnp.pad(a, ((0, rows - a.shape[0]), (0, 0)))


def _layer_weights(p):
    w_in = p["w_in"]
    gate_lanes = jnp.concatenate([jnp.zeros((2, 2 * DN_HEADS), F32),
                                  jnp.stack([p["dn_a_log"].reshape(-1), p["dn_dt_bias"].reshape(-1)])], axis=1)
    return {
        "g_mix": p["norm_mix"].reshape(1, D_MODEL),
        "g_ffn": p["norm_ffn"].reshape(1, D_MODEL),
        "w_main": w_in[:, :D_MAIN].astype(BF16),
        "w_small": jnp.pad(w_in[:, D_MAIN:], ((0, 0), (0, LANES - 4 * DN_HEADS))).astype(BF16),
        "gains": _pad_rows(jnp.concatenate([p["na_qk_gain"], p["gqa_qk_gain"]], axis=0)),
        "gate_params": _pad_rows(jnp.pad(gate_lanes, ((0, 0), (0, LANES - 4 * DN_HEADS)))),
        "w_out": p["w_out"].astype(BF16),
        "conv_w": _pad_rows(p["dn_conv"]),
        "out_gain": _pad_rows(p["dn_out_gain"].reshape(1, HEAD_DIM)),
        "router_w": jnp.pad(p["router_w"], ((0, 0), (0, LANES - N_EXPERTS))),
    }


def _hybrid_layer(x, ctx, mod_lat, mod_ctx, p, experts, layer, tabs, need_ctx_out):
    W = _layer_weights(p)
    M = ctx.shape[1]
    proj = functools.partial(_project, g_mix=W["g_mix"], w_main=W["w_main"], w_small=W["w_small"],
                             gains=W["gains"], gate_params=W["gate_params"])
    p_lat, gates_lat = proj(x, mod_lat, cos_t=tabs[0], sin_t=tabs[1], rope=True, tm=1024, per_batch_mod=True)
    p_ctx, gates_ctx = proj(ctx, mod_ctx, cos_t=tabs[0][:M], sin_t=tabs[1][:M], rope=False, tm=M,
                            per_batch_mod=False)
    oa = _na_attention(p_lat, p_ctx, _na_bias_tables(p["na_rpb"], x.shape[1] // GRID_W))
    ob = _gqa_attention(p_lat, p_ctx)
    o_f, o_b = _deltanet(p_ctx, p_lat, gates_ctx, gates_lat, W["conv_w"])
    moe = functools.partial(_expert_choice, g_ffn=W["g_ffn"], router_w=W["router_w"], w_gate=experts[0],
                            w_up=experts[1], w_down=experts[2], layer=layer)
    x = _out_project(oa, ob, o_f, o_b, p_lat, x, mod_lat, W["out_gain"], W["w_out"],
                     dn_row_off=M, per_batch_mod=True)
    x = moe(x, mod_lat, per_batch_mod=True)
    if not need_ctx_out:
        return x, None
    oa_c, ob_c = _ctx_attention(p_ctx)
    ctx = _out_project(oa_c, ob_c, o_f, o_b, p_ctx, ctx, mod_ctx, W["out_gain"], W["w_out"],
                       dn_row_off=0, per_batch_mod=False)
    ctx = moe(ctx, mod_ctx, per_batch_mod=False)
    return x, ctx


def kernel(x, c, ctx, c_ctx, ada_w, ada_b, norm_mix, norm_ffn, w_in, w_out, na_qk_gain, na_rpb, gqa_qk_gain, dn_conv, dn_a_log, dn_dt_bias, dn_out_gain, router_w, exp_w_gate, exp_w_up, exp_w_down):
    B = x.shape[0]
    depth = ada_w.shape[0]
    c_rows = _pad_rows(jnp.concatenate([c, c_ctx[None, :]], axis=0))
    mods = _modulation(c_rows, ada_w, ada_b.reshape(depth, 1, -1)).reshape(depth, 8, N_MOD, D_MODEL)
    mods = jnp.pad(mods, ((0, 0), (0, 0), (0, 8 - N_MOD), (0, 0)))
    tabs = _rope_tables(x.shape[1])
    params = dict(norm_mix=norm_mix, norm_ffn=norm_ffn, w_in=w_in, w_out=w_out, na_qk_gain=na_qk_gain,
                  na_rpb=na_rpb, gqa_qk_gain=gqa_qk_gain, dn_conv=dn_conv, dn_a_log=dn_a_log,
                  dn_dt_bias=dn_dt_bias, dn_out_gain=dn_out_gain, router_w=router_w)
    experts = (exp_w_gate, exp_w_up, exp_w_down)
    for l in range(depth):
        p = {k: v[l] for k, v in params.items()}
        x, ctx = _hybrid_layer(x, ctx, mods[l, :B], mods[l, B:B + 1], p, experts, l, tabs,
                               need_ctx_out=l < depth - 1)
    return x
```

```python
import functools
import math

import jax
import jax.numpy as jnp
import numpy as np
from jax import lax
from jax.experimental import pallas as pl
from jax.experimental.pallas import tpu as pltpu

D_MODEL = 2048
SEQ = 4096
DEPTH = 4
GRID_W = 64
CTX_LEN = 256
HEAD_DIM = 128
NA_HEADS = 4
NA_WIN_R = 8
NA_WIN_C = 16
GQA_Q_HEADS = 8
GQA_KV_HEADS = 2
ROPE_THETA = 10000.0
DN_HEADS = 4
DN_CONV = 5
W_A = NA_HEADS * HEAD_DIM
W_B = GQA_Q_HEADS * HEAD_DIM
W_KV = GQA_KV_HEADS * HEAD_DIM
W_C = DN_HEADS * HEAD_DIM
MIX_WIDTH = W_A + W_B + W_C
D_MAIN = 3 * W_A + W_B + 2 * W_KV + 4 * W_C
N_EXPERTS = 16
EC_FACTOR = 2
D_EXPERT = 1024
N_MOD = 6
NORM_EPS = 1e-6
NEG_INF = -1e30
F32 = jnp.float32
BF16 = jnp.bfloat16

LANES = 128
VMEM_LIMIT = 56 * 1024 * 1024
LOG2E = 1.4426950408889634


def _cparams(sem):
    return pltpu.CompilerParams(dimension_semantics=sem, vmem_limit_bytes=VMEM_LIMIT)


def _silu(x):
    return x * (1.0 / (1.0 + jnp.exp(-x)))


MOD_TN = 1024


def _mod_kernel(c_ref, w_ref, b_ref, o_ref):
    c = _silu(c_ref[...]).astype(BF16)
    o_ref[...] = jnp.dot(c, w_ref[...].astype(BF16), preferred_element_type=F32) + b_ref[...]


def _modulation(c_rows, ada_w, ada_b):
    n_out = N_MOD * D_MODEL
    depth = ada_w.shape[0]
    return pl.pallas_call(
        _mod_kernel,
        grid=(depth, n_out // MOD_TN),
        in_specs=[pl.BlockSpec((8, D_MODEL), lambda l, j: (0, 0)),
                  pl.BlockSpec((None, D_MODEL, MOD_TN), lambda l, j: (l, 0, j)),
                  pl.BlockSpec((None, 1, MOD_TN), lambda l, j: (l, 0, j))],
        out_specs=pl.BlockSpec((None, 8, MOD_TN), lambda l, j: (l, 0, j)),
        out_shape=jax.ShapeDtypeStruct((depth, 8, n_out), F32),
        compiler_params=_cparams(("parallel", "parallel")),
        name="adaln_mod",
    )(c_rows, ada_w, ada_b)


PROJ_TN = 512
N_PROJ_TILES = D_MAIN // PROJ_TN


def _ada_norm_rows(x, gain, shift, scale):
    y = x * lax.rsqrt(jnp.mean(x * x, axis=-1, keepdims=True) + NORM_EPS)
    return (y * gain) * (1.0 + scale) + shift


def _head_norm(a, gain):
    return a * lax.rsqrt(jnp.mean(a * a, axis=-1, keepdims=True) + NORM_EPS) * gain


def _rope(a, cos, sin):
    lane = lax.broadcasted_iota(jnp.int32, a.shape, 1)
    partner = jnp.where((lane % 64) < 32, pltpu.roll(a, LANES - 32, 1), pltpu.roll(a, 32, 1))
    return a * cos + partner * sin


def _proj_kernel(x_ref, mod_ref, g_ref, w_ref, ws_ref, gains_ref, gp_ref, cos_ref, sin_ref,
                 o_ref, gates_ref, h_ref, *, rope, tm):
    j = pl.program_id(2)

    @pl.when(j == 0)
    def _():
        h = _ada_norm_rows(x_ref[...], g_ref[...], mod_ref[0:1, :], mod_ref[1:2, :])
        hb = h.astype(BF16)
        h_ref[...] = hb
        raw = jnp.dot(hb, ws_ref[...], preferred_element_type=F32)
        lane = lax.broadcasted_iota(jnp.int32, raw.shape, 1)
        beta = 1.0 / (1.0 + jnp.exp(-raw))
        z = raw + gp_ref[1:2, :]
        softplus = jnp.maximum(z, 0.0) + jnp.log1p(jnp.exp(-jnp.abs(z)))
        gates_ref[...] = jnp.where(lane < 2 * DN_HEADS, beta, -jnp.exp(gp_ref[0:1, :]) * softplus)

    acc = jnp.dot(h_ref[...], w_ref[...], preferred_element_type=F32)

    def normed(col, gi, with_rope):
        a = _head_norm(acc[:, col * LANES:(col + 1) * LANES], gains_ref[gi:gi + 1, :])
        if with_rope and rope:
            a = _rope(a, cos_ref[...], sin_ref[...])
        return a.astype(BF16)

    @pl.when(j < 2)
    def _():
        gain = gains_ref[pl.ds(j, 1), :]
        for col in range(4):
            o_ref[:, col * LANES:(col + 1) * LANES] = _head_norm(
                acc[:, col * LANES:(col + 1) * LANES], gain).astype(BF16)

    @pl.when(jnp.logical_or(j == 3, j == 4))
    def _():
        for col in range(4):
            o_ref[:, col * LANES:(col + 1) * LANES] = normed(col, 2, True)

    @pl.when(j == 5)
    def _():
        for col in range(2):
            o_ref[:, col * LANES:(col + 1) * LANES] = normed(col, 3, True)
        o_ref[:, 2 * LANES:] = acc[:, 2 * LANES:].astype(BF16)

    @pl.when(jnp.logical_or(j == 2, j > 5))
    def _():
        o_ref[...] = acc.astype(BF16)


def _project(x, mod, g_mix, w_main, w_small, gains, gate_params, cos_t, sin_t, *, rope, tm, per_batch_mod):
    B, N, _ = x.shape
    mod_idx = (lambda b, i, j: (b, 0, 0)) if per_batch_mod else (lambda b, i, j: (0, 0, 0))
    return pl.pallas_call(
        functools.partial(_proj_kernel, rope=rope, tm=tm),
        grid=(B, N // tm, N_PROJ_TILES),
        in_specs=[pl.BlockSpec((None, tm, D_MODEL), lambda b, i, j: (b, i, 0)),
                  pl.BlockSpec((None, 8, D_MODEL), mod_idx),
                  pl.BlockSpec((1, D_MODEL), lambda b, i, j: (0, 0)),
                  pl.BlockSpec((D_MODEL, PROJ_TN), lambda b, i, j: (0, j)),
                  pl.BlockSpec((D_MODEL, LANES), lambda b, i, j: (0, 0)),
                  pl.BlockSpec((8, LANES), lambda b, i, j: (0, 0)),
                  pl.BlockSpec((8, LANES), lambda b, i, j: (0, 0)),
                  pl.BlockSpec((tm, LANES), lambda b, i, j: (i, 0)),
                  pl.BlockSpec((tm, LANES), lambda b, i, j: (i, 0))],
        out_specs=[pl.BlockSpec((None, tm, PROJ_TN), lambda b, i, j: (b, i, j)),
                   pl.BlockSpec((None, tm, LANES), lambda b, i, j: (b, i, 0))],
        out_shape=[jax.ShapeDtypeStruct((B, N, D_MAIN), BF16),
                   jax.ShapeDtypeStruct((B, N, LANES), F32)],
        scratch_shapes=[pltpu.VMEM((tm, D_MODEL), BF16)],
        compiler_params=_cparams(("parallel", "parallel", "arbitrary")),
        name="in_proj",
    )(x, mod, g_mix, w_main, w_small, gains, gate_params, cos_t, sin_t)


SM_SCALE = HEAD_DIM ** -0.5
_NT = (((1,), (1,)), ((), ()))
COL_QA, COL_KA, COL_VA = 0, W_A // LANES, 2 * W_A // LANES
COL_QB = 3 * W_A // LANES
COL_KB = COL_QB + W_B // LANES
COL_VB = COL_KB + W_KV // LANES
COL_QC = COL_VB + W_KV // LANES
COL_GC = COL_QC + 3 * W_C // LANES


def _softmax_pv(score_blocks, value_blocks):
    m = functools.reduce(jnp.maximum, [jnp.max(s, axis=-1, keepdims=True) for s in score_blocks])
    ps = [jnp.exp2((s - m) * (SM_SCALE * LOG2E)) for s in score_blocks]
    l = functools.reduce(jnp.add, [jnp.sum(p, axis=-1, keepdims=True) for p in ps])
    o = functools.reduce(jnp.add, [jnp.dot(p.astype(BF16), v, preferred_element_type=F32)
                                   for p, v in zip(ps, value_blocks)])
    return o / l


GQA_TQ = 512
GQA_GROUP = GQA_Q_HEADS // GQA_KV_HEADS


GQA_TK = 2048


def _gqa_kernel(q_ref, kl_ref, vl_ref, kc_ref, vc_ref, o_ref):
    n_lat = kl_ref.shape[0]
    c_exp = SM_SCALE * LOG2E
    for r in range(GQA_GROUP):
        q = q_ref[:, r * LANES:(r + 1) * LANES]
        m = l = acc = None
        for c0 in [None] + list(range(0, n_lat, GQA_TK)):
            k = kc_ref[...] if c0 is None else kl_ref[c0:c0 + GQA_TK, :]
            v = vc_ref[...] if c0 is None else vl_ref[c0:c0 + GQA_TK, :]
            s = lax.dot_general(q, k, _NT, preferred_element_type=F32)
            m_c = jnp.max(s, axis=-1, keepdims=True)
            if m is None:
                m = m_c
                p = jnp.exp2((s - m) * c_exp)
                l = jnp.sum(p, axis=-1, keepdims=True)
                acc = jnp.dot(p.astype(BF16), v, preferred_element_type=F32)
            else:
                m_new = jnp.maximum(m, m_c)
                alpha = jnp.exp2((m - m_new) * c_exp)
                p = jnp.exp2((s - m_new) * c_exp)
                l = alpha * l + jnp.sum(p, axis=-1, keepdims=True)
                acc = alpha * acc + jnp.dot(p.astype(BF16), v, preferred_element_type=F32)
                m = m_new
        o_ref[:, r * LANES:(r + 1) * LANES] = (acc / l).astype(BF16)


def _gqa_attention(p_lat, p_ctx):
    B, N, _ = p_lat.shape
    M = p_ctx.shape[1]
    qw = GQA_GROUP * LANES
    return pl.pallas_call(
        _gqa_kernel,
        grid=(B, GQA_KV_HEADS, N // GQA_TQ),
        in_specs=[pl.BlockSpec((None, GQA_TQ, qw), lambda b, g, i: (b, i, COL_QB // GQA_GROUP + g)),
                  pl.BlockSpec((None, N, LANES), lambda b, g, i: (b, 0, COL_KB + g)),
                  pl.BlockSpec((None, N, LANES), lambda b, g, i: (b, 0, COL_VB + g)),
                  pl.BlockSpec((None, M, LANES), lambda b, g, i: (b, 0, COL_KB + g)),
                  pl.BlockSpec((None, M, LANES), lambda b, g, i: (b, 0, COL_VB + g))],
        out_specs=pl.BlockSpec((None, GQA_TQ, qw), lambda b, g, i: (b, i, g)),
        out_shape=jax.ShapeDtypeStruct((B, N, W_B), BF16),
        compiler_params=_cparams(("parallel", "parallel", "parallel")),
        name="gqa_attention",
    )(p_lat, p_lat, p_lat, p_ctx, p_ctx)


def _ctx_attn_kernel(p_ref, oa_ref, ob_ref):
    def col(c):
        return p_ref[:, c * LANES:(c + 1) * LANES]
    for h in range(NA_HEADS):
        s = lax.dot_general(col(COL_QA + h), col(COL_KA + h), _NT, preferred_element_type=F32)
        oa_ref[:, h * LANES:(h + 1) * LANES] = _softmax_pv([s], [col(COL_VA + h)]).astype(BF16)
    for h in range(GQA_Q_HEADS):
        g = h // GQA_GROUP
        s = lax.dot_general(col(COL_QB + h), col(COL_KB + g), _NT, preferred_element_type=F32)
        ob_ref[:, h * LANES:(h + 1) * LANES] = _softmax_pv([s], [col(COL_VB + g)]).astype(BF16)


def _ctx_attention(p_ctx):
    B, M, _ = p_ctx.shape
    return pl.pallas_call(
        _ctx_attn_kernel,
        grid=(B,),
        in_specs=[pl.BlockSpec((None, M, D_MAIN), lambda b: (b, 0, 0))],
        out_specs=[pl.BlockSpec((None, M, W_A), lambda b: (b, 0, 0)),
                   pl.BlockSpec((None, M, W_B), lambda b: (b, 0, 0))],
        out_shape=[jax.ShapeDtypeStruct((B, M, W_A), BF16), jax.ShapeDtypeStruct((B, M, W_B), BF16)],
        compiler_params=_cparams(("parallel",)),
        name="ctx_attention",
    )(p_ctx)


NA_ROWS = 8
NA_KROWS = 16
NA_TQ = NA_ROWS * GRID_W
NA_TK = NA_KROWS * GRID_W


NA_HPS = 2


def _na_kernel(q_ref, k_ref, v_ref, kc_ref, vc_ref, bias_ref, o_ref, *, n_rows):
    g = pl.program_id(2)
    kbase = jnp.clip(g * NA_ROWS - NA_WIN_R // 2, 0, n_rows - NA_KROWS)
    off = pl.multiple_of(kbase * GRID_W, GRID_W)
    for h in range(NA_HPS):
        lanes = slice(h * LANES, (h + 1) * LANES)
        q = q_ref[:, lanes]
        k = k_ref[pl.ds(off, NA_TK), lanes]
        v = v_ref[pl.ds(off, NA_TK), lanes]
        s_loc = lax.dot_general(q, k, _NT, preferred_element_type=F32) + bias_ref[h]
        s_ctx = lax.dot_general(q, kc_ref[:, lanes], _NT, preferred_element_type=F32)
        o_ref[:, lanes] = _softmax_pv([s_loc, s_ctx], [v, vc_ref[:, lanes]]).astype(BF16)


def _na_bias_tables(rpb, n_rows):
    wr = min(NA_WIN_R, n_rows)
    qc = np.arange(GRID_W)
    cstart = np.clip(qc - NA_WIN_C // 2, 0, GRID_W - NA_WIN_C)
    in_col = (qc[None, :] >= cstart[:, None]) & (qc[None, :] < cstart[:, None] + NA_WIN_C)
    dc = np.clip(qc[None, :] - qc[:, None] + (NA_WIN_C - 1), 0, 2 * NA_WIN_C - 2)
    pick = (dc[..., None] == np.arange(2 * NA_WIN_C - 1)).astype(np.float32)
    cols = jnp.einsum('lhab,qkb->lhaqk', rpb.astype(F32), pick, precision=lax.Precision.HIGHEST)
    cols = jnp.where(in_col, cols / SM_SCALE, NEG_INF)
    masked = jnp.full(cols.shape[:2] + (GRID_W, GRID_W), NEG_INF, F32)
    tabs = []
    for g in (0, 1, n_rows // NA_ROWS - 1):
        r0 = g * NA_ROWS
        kbase = min(max(r0 - NA_WIN_R // 2, 0), n_rows - NA_KROWS)
        rows = []
        for r in range(NA_ROWS):
            qr = r0 + r
            rstart = min(max(qr - wr // 2, 0), n_rows - wr)
            blocks = []
            for d in range(NA_KROWS):
                kr = kbase + d
                blocks.append(cols[:, :, kr - qr + NA_WIN_R - 1] if rstart <= kr < rstart + wr else masked)
            rows.append(jnp.concatenate(blocks, axis=-1))
        tabs.append(jnp.concatenate(rows, axis=-2))
    return jnp.stack(tabs, axis=2)


def _na_attention(p_lat, p_ctx, bias_tabs, layer):
    B, N, _ = p_lat.shape
    M = p_ctx.shape[1]
    n_rows = N // GRID_W
    n_groups = n_rows // NA_ROWS
    variant = lambda g: jnp.where(g == 0, 0, jnp.where(g == n_groups - 1, 2, 1))
    hw = NA_HPS * LANES
    return pl.pallas_call(
        functools.partial(_na_kernel, n_rows=n_rows),
        grid=(B, NA_HEADS // NA_HPS, n_groups),
        in_specs=[pl.BlockSpec((None, NA_TQ, hw), lambda b, h, g: (b, g, COL_QA // NA_HPS + h)),
                  pl.BlockSpec((None, N, hw), lambda b, h, g: (b, 0, COL_KA // NA_HPS + h)),
                  pl.BlockSpec((None, N, hw), lambda b, h, g: (b, 0, COL_VA // NA_HPS + h)),
                  pl.BlockSpec((None, M, hw), lambda b, h, g: (b, 0, COL_KA // NA_HPS + h)),
                  pl.BlockSpec((None, M, hw), lambda b, h, g: (b, 0, COL_VA // NA_HPS + h)),
                  pl.BlockSpec((None, NA_HPS, None, NA_TQ, NA_TK), lambda b, h, g: (layer, h, variant(g), 0, 0))],
        out_specs=pl.BlockSpec((None, NA_TQ, hw), lambda b, h, g: (b, g, h)),
        out_shape=jax.ShapeDtypeStruct((B, N, W_A), BF16),
        compiler_params=_cparams(("parallel", "parallel", "parallel")),
        name="na_attention",
    )(p_lat, p_lat, p_lat, p_ctx, p_ctx, bias_tabs)


DN_L = 128
DN_HALF = DN_L // 2
DN_CHAINS = 2 * DN_HEADS
_TN = (((0,), (0,)), ((), ()))


def _dn_conv_kernel(xc_ref, xl_ref, w_ref, o_ref, pad_ref, *, n_ctx, n_lat):
    which = pl.program_id(1)
    half = DN_CONV // 2
    q_scale = jnp.where(which == 0, HEAD_DIM ** -0.5, 1.0)

    def run(x_ref, n, out_off):
        pad_ref[0:8, :] = jnp.zeros((8, LANES), F32)
        pad_ref[8 + n:16 + n, :] = jnp.zeros((8, LANES), F32)
        pad_ref[8:8 + n, :] = x_ref[...].astype(F32)
        chunk = min(n, 256)
        for c0 in range(0, n, chunk):
            y = jnp.zeros((chunk, LANES), F32)
            for i in range(DN_CONV):
                y = y + w_ref[i:i + 1, :] * pad_ref[8 + c0 + i - half:8 + c0 + i - half + chunk, :]
            y = _silu(y)
            yn = y * lax.rsqrt(jnp.sum(y * y, axis=-1, keepdims=True) + NORM_EPS) * q_scale
            o_ref[out_off + c0:out_off + c0 + chunk, :] = jnp.where(which < 2, yn, y)

    run(xc_ref, n_ctx, 0)
    run(xl_ref, n_lat, n_ctx)


def _dn_conv(p_ctx, p_lat, conv_w):
    B, N, _ = p_lat.shape
    M = p_ctx.shape[1]
    col = lambda b, s, h: (b, 0, COL_QC + s * DN_HEADS + h)
    return pl.pallas_call(
        functools.partial(_dn_conv_kernel, n_ctx=M, n_lat=N),
        grid=(B, 3, DN_HEADS),
        in_specs=[pl.BlockSpec((None, M, LANES), col),
                  pl.BlockSpec((None, N, LANES), col),
                  pl.BlockSpec((8, LANES), lambda b, s, h: (0, s * DN_HEADS + h))],
        out_specs=pl.BlockSpec((None, None, M + N, LANES), lambda b, s, h: (b, s, 0, h)),
        out_shape=jax.ShapeDtypeStruct((B, 3, M + N, W_C), F32),
        scratch_shapes=[pltpu.VMEM((N + 16, LANES), F32)],
        compiler_params=_cparams(("parallel", "parallel", "parallel")),
        name="dn_conv",
    )(p_ctx, p_lat, conv_w)


def _scan_incl(x, axis):
    idx = lax.broadcasted_iota(jnp.int32, x.shape, axis)
    s = 1
    while s < DN_L:
        x = x + jnp.where(idx >= s, pltpu.roll(x, s, axis), 0.0)
        s *= 2
    return x


def _dn_gate_sums(gates, gates_t):
    pf = _scan_incl(gates, 0)
    tot = pf[DN_L - 1:DN_L, :]
    pf_t = _scan_incl(gates_t, 1)
    tot_t = pf_t[:, DN_L - 1:DN_L]
    return (pf, tot - pf + gates), (pf_t, tot_t - pf_t + gates_t), tot


def _dn_chain_terms(h, d, gates, gates_t, sums):
    (cum, cum_t, tot) = sums
    lb = d * DN_HEADS + h
    lg = 2 * DN_HEADS + lb
    row = lax.broadcasted_iota(jnp.int32, (DN_L, DN_L), 0)
    col = lax.broadcasted_iota(jnp.int32, (DN_L, DN_L), 1)
    return (gates[:, lb:lb + 1], gates_t[lb:lb + 1, :], cum[d][:, lg:lg + 1], cum_t[d][lg:lg + 1, :],
            tot[:, lg:lg + 1], row, col)


def _dn_lower_form(h, d, gram, gates, gates_t, sums):
    beta_c, beta_r, g_c, g_r, _, row, col = _dn_chain_terms(h, d, gates, gates_t, sums)
    if d == 0:
        m = beta_c * gram * jnp.exp(jnp.where(row > col, g_c - g_r, NEG_INF))
    else:
        m = beta_r * gram * jnp.exp(jnp.where(row > col, g_r - g_c, NEG_INF))
    return jnp.where(row > col, m, 0.0)


def _dn_gram_kernel(k_ref, g_ref, gt_ref, m_ref):
    gates, gates_t = g_ref[...], gt_ref[...]
    sums = _dn_gate_sums(gates, gates_t)
    for h in range(DN_HEADS):
        kb = k_ref[:, h * LANES:(h + 1) * LANES].astype(BF16)
        gram = lax.dot_general(kb, kb, _NT, preferred_element_type=F32)
        for d in range(2):
            m_ref[d * DN_HEADS + h] = _dn_lower_form(h, d, gram, gates, gates_t, sums)


def _dn_gram(dnx, gates, gates_t):
    B, _, T, _ = dnx.shape
    n_sc = T // DN_L
    return pl.pallas_call(
        _dn_gram_kernel,
        grid=(B, n_sc),
        in_specs=[pl.BlockSpec((None, None, DN_L, W_C), lambda b, s: (b, 1, s, 0)),
                  pl.BlockSpec((None, DN_L, LANES), lambda b, s: (b, s, 0)),
                  pl.BlockSpec((None, 16, DN_L), lambda b, s: (b, 0, s))],
        out_specs=pl.BlockSpec((None, None, DN_CHAINS, DN_L, DN_L), lambda b, s: (b, s, 0, 0, 0)),
        out_shape=jax.ShapeDtypeStruct((B, n_sc, DN_CHAINS, DN_L, DN_L), F32),
        compiler_params=_cparams(("parallel", "parallel")),
        name="dn_gram",
    )(dnx, gates, gates_t)


TRI_GROUP = LANES // 2


def _tri_inv_kernel(m_ref, t_ref, a_scr, t_scr):
    n = DN_HALF
    lane_n = lax.broadcasted_iota(jnp.int32, (n, LANES), 1)
    for i in range(n):
        rt = jnp.concatenate([m_ref[:, i, :], m_ref[:, n + i, :]], axis=0).T
        a_scr[i * n:(i + 1) * n, :] = jnp.where(lane_n < n, rt[:n], rt[n:])

    t_scr[...] = jnp.zeros(t_scr.shape, F32)
    col_id = lax.broadcasted_iota(jnp.int32, (n, LANES), 0)

    def row_step(i, carry):
        def blk(jb, acc):
            for jj in range(8):
                j = jb * 8 + jj
                a = a_scr[pl.ds(i * n + j, 1), :]
                acc = acc - a * t_scr[pl.ds(pl.multiple_of(j * n, n), n), :]
            return acc
        acc = lax.fori_loop(0, (i + 7) // 8, blk, jnp.where(col_id == i, 1.0, 0.0))
        t_scr[pl.ds(pl.multiple_of(i * n, n), n), :] = acc
        return carry

    lax.fori_loop(0, n, row_step, 0)

    for i in range(n):
        slab = t_scr[i * n:(i + 1) * n, :]
        st = jnp.concatenate([slab, slab], axis=0).T
        t_ref[:, i, :] = jnp.where(lane_n < n, st[:n], 0.0)
        t_ref[:, n + i, :] = jnp.where(lane_n >= n, st[n:], 0.0)


def _tri_inv(m_low):
    n = m_low.shape[0]
    assert n % TRI_GROUP == 0, n
    return pl.pallas_call(
        _tri_inv_kernel,
        grid=(n // TRI_GROUP,),
        in_specs=[pl.BlockSpec((TRI_GROUP, DN_L, DN_L), lambda g: (g, 0, 0))],
        out_specs=pl.BlockSpec((TRI_GROUP, DN_L, DN_L), lambda g: (g, 0, 0)),
        out_shape=jax.ShapeDtypeStruct(m_low.shape, F32),
        scratch_shapes=[pltpu.VMEM((DN_HALF * DN_HALF, LANES), F32), pltpu.VMEM((DN_HALF * DN_HALF, LANES), F32)],
        compiler_params=_cparams(("parallel",)),
        name="dn_tri_inv",
    )(m_low)


def _dn_prep_kernel(q_ref, k_ref, v_ref, g_ref, gt_ref, m_ref, t_ref, u_ref, ws_ref, wv_ref, gl_ref):
    gates, gates_t = g_ref[...], gt_ref[...]
    sums = _dn_gate_sums(gates, gates_t)
    H = DN_HALF
    for h in range(DN_HEADS):
        q = q_ref[:, h * LANES:(h + 1) * LANES]
        k = k_ref[:, h * LANES:(h + 1) * LANES]
        v = v_ref[:, h * LANES:(h + 1) * LANES]
        qk_raw = lax.dot_general(q.astype(BF16), k.astype(BF16), _NT, preferred_element_type=F32)
        for d in range(2):
            c = d * DN_HEADS + h
            beta_c, _, g_c, g_r, g_tot, row, col = _dn_chain_terms(h, d, gates, gates_t, sums)
            causal = (row >= col) if d == 0 else (row <= col)
            decay = jnp.exp(jnp.where(causal, g_c - g_r, NEG_INF))
            eg = jnp.exp(g_c)
            rhs = jnp.concatenate([v * beta_c, k * (beta_c * eg)], axis=1)
            t_bd = t_ref[c].astype(BF16)
            m21 = jnp.where(jnp.logical_and(row >= H, col < H), m_ref[c], 0.0).astype(BF16)
            apply = (lambda a, b: jnp.dot(a, b.astype(BF16), preferred_element_type=F32)) if d == 0 else (
                lambda a, b: lax.dot_general(a, b.astype(BF16), _TN, preferred_element_type=F32))
            first = apply(t_bd, rhs)
            sol = apply(t_bd, rhs - apply(m21, first))
            u_ref[c] = sol[:, :LANES]
            ws_ref[c, :DN_L, :] = sol[:, LANES:].astype(BF16)
            ws_ref[c, DN_L:, :] = (q * eg).astype(BF16)
            wv_ref[c, :DN_L, :] = (qk_raw * decay).astype(BF16)
            wv_ref[c, DN_L:, :] = (k * jnp.exp(g_tot - g_c)).T.astype(BF16)
            gl_ref[c] = jnp.broadcast_to(jnp.exp(g_tot), (8, LANES))


def _dn_prep(dnx, gates, gates_t, m_low, t_inv):
    B, _, T, _ = dnx.shape
    n_sc = T // DN_L
    blk = pl.BlockSpec((None, None, DN_CHAINS, DN_L, DN_L), lambda b, s: (b, s, 0, 0, 0))
    blk2 = pl.BlockSpec((None, None, DN_CHAINS, 2 * DN_L, DN_L), lambda b, s: (b, s, 0, 0, 0))
    stacked = jax.ShapeDtypeStruct((B, n_sc, DN_CHAINS, 2 * DN_L, DN_L), BF16)
    return pl.pallas_call(
        _dn_prep_kernel,
        grid=(B, n_sc),
        in_specs=[pl.BlockSpec((None, None, DN_L, W_C), lambda b, s: (b, 0, s, 0)),
                  pl.BlockSpec((None, None, DN_L, W_C), lambda b, s: (b, 1, s, 0)),
                  pl.BlockSpec((None, None, DN_L, W_C), lambda b, s: (b, 2, s, 0)),
                  pl.BlockSpec((None, DN_L, LANES), lambda b, s: (b, s, 0)),
                  pl.BlockSpec((None, 16, DN_L), lambda b, s: (b, 0, s)),
                  blk, blk],
        out_specs=[blk, blk2, blk2,
                   pl.BlockSpec((None, None, DN_CHAINS, 8, LANES), lambda b, s: (b, s, 0, 0, 0))],
        out_shape=[jax.ShapeDtypeStruct((B, n_sc, DN_CHAINS, DN_L, DN_L), F32), stacked, stacked,
                   jax.ShapeDtypeStruct((B, n_sc, DN_CHAINS, 8, LANES), F32)],
        compiler_params=_cparams(("parallel", "parallel")),
        name="dn_prep",
    )(dnx, dnx, dnx, gates, gates_t, m_low, t_inv)


def _dn_scan_kernel(uf, wsf, wvf, glf, ub, wsb, wvb, glb, of_ref, ob_ref, s_ref):
    @pl.when(pl.program_id(1) == 0)
    def _():
        s_ref[...] = jnp.zeros(s_ref.shape, F32)

    for d, (u, ws, wv, gl, o_ref) in enumerate(((uf, wsf, wvf, glf, of_ref), (ub, wsb, wvb, glb, ob_ref))):
        for h in range(DN_HEADS):
            c = d * DN_HEADS + h
            s = s_ref[c]
            on_s = jnp.dot(ws[h], s.astype(BF16), preferred_element_type=F32)
            v_new = u[h] - on_s[:DN_L]
            on_v = jnp.dot(wv[h], v_new.astype(BF16), preferred_element_type=F32)
            s_ref[c] = s * gl[h][0:1, :] + on_v[DN_L:]
            o_ref[:, h * LANES:(h + 1) * LANES] = on_s[DN_L:] + on_v[:DN_L]


def _dn_scan(u, ws, wv, gl, n_ctx_chunks):
    B, n_sc = u.shape[0], u.shape[1]
    fwd = lambda b, s: s
    bwd = lambda b, s: jnp.where(s < n_ctx_chunks, n_ctx_chunks - 1 - s, n_sc - 1 + n_ctx_chunks - s)
    specs = []
    for d, walk in ((0, fwd), (1, bwd)):
        for rows in (DN_L, 2 * DN_L, 2 * DN_L, 8):
            specs.append(pl.BlockSpec((None, None, DN_HEADS, rows, LANES),
                                      functools.partial(lambda b, s, d, walk: (b, walk(b, s), d, 0, 0), d=d, walk=walk)))
    out = jax.ShapeDtypeStruct((B, n_sc * DN_L, W_C), F32)
    return pl.pallas_call(
        _dn_scan_kernel,
        grid=(B, n_sc),
        in_specs=specs,
        out_specs=[pl.BlockSpec((None, DN_L, W_C), lambda b, s: (b, fwd(b, s), 0)),
                   pl.BlockSpec((None, DN_L, W_C), lambda b, s: (b, bwd(b, s), 0))],
        out_shape=[out, out],
        scratch_shapes=[pltpu.VMEM((DN_CHAINS, DN_L, DN_L), F32)],
        compiler_params=_cparams(("parallel", "arbitrary")),
        name="dn_scan",
    )(u, ws, wv, gl, u, ws, wv, gl)


def _deltanet(p_ctx, p_lat, gates_ctx, gates_lat, conv_w):
    B = p_lat.shape[0]
    n_ctx_chunks = p_ctx.shape[1] // DN_L
    dnx = _dn_conv(p_ctx, p_lat, conv_w)
    gates = jnp.concatenate([gates_ctx, gates_lat], axis=1)
    gates_t = jnp.swapaxes(gates[:, :, :16], 1, 2)
    m_low = _dn_gram(dnx, gates, gates_t)
    t_inv = _tri_inv(m_low.reshape(-1, DN_L, DN_L)).reshape(m_low.shape)
    u, ws, wv, gl = _dn_prep(dnx, gates, gates_t, m_low, t_inv)
    return _dn_scan(u, ws, wv, gl, n_ctx_chunks)


OUT_TM = 256


def _out_proj_kernel(oa_ref, ob_ref, of_ref, obk_ref, gc_ref, x_ref, mod_ref, gain_ref, w_ref, o_ref, o2_ref):
    o_dn = of_ref[...] + obk_ref[...]
    acc = jnp.dot(oa_ref[...], w_ref[0:W_A, :], preferred_element_type=F32)
    acc = acc + jnp.dot(ob_ref[...], w_ref[W_A:W_A + W_B, :], preferred_element_type=F32)
    for h in range(DN_HEADS):
        oh = _head_norm(o_dn[:, h * LANES:(h + 1) * LANES], gain_ref[0:1, :])
        gate = gc_ref[:, h * LANES:(h + 1) * LANES].astype(F32)
        oc = (oh * _silu(gate)).astype(BF16)
        lo = W_A + W_B + h * LANES
        acc = acc + jnp.dot(oc, w_ref[lo:lo + LANES, :], preferred_element_type=F32)
    y = x_ref[...] + mod_ref[2:3, :] * acc
    o_ref[...] = y
    o2_ref[...] = y


def _out_project(oa, ob, o_f, o_b, proj, x, mod, out_gain, w_out, *, dn_row_off, per_batch_mod):
    B, N, _ = x.shape
    tm = OUT_TM
    off = dn_row_off // tm
    mod_idx = (lambda b, i: (b, 0, 0)) if per_batch_mod else (lambda b, i: (0, 0, 0))
    return pl.pallas_call(
        _out_proj_kernel,
        grid=(B, N // tm),
        in_specs=[pl.BlockSpec((None, tm, W_A), lambda b, i: (b, i, 0)),
                  pl.BlockSpec((None, tm, W_B), lambda b, i: (b, i, 0)),
                  pl.BlockSpec((None, tm, W_C), lambda b, i: (b, i + off, 0)),
                  pl.BlockSpec((None, tm, W_C), lambda b, i: (b, i + off, 0)),
                  pl.BlockSpec((None, tm, W_C), lambda b, i: (b, i, COL_GC * LANES // W_C)),
                  pl.BlockSpec((None, tm, D_MODEL), lambda b, i: (b, i, 0)),
                  pl.BlockSpec((None, 8, D_MODEL), mod_idx),
                  pl.BlockSpec((8, LANES), lambda b, i: (0, 0)),
                  pl.BlockSpec((MIX_WIDTH, D_MODEL), lambda b, i: (0, 0))],
        out_specs=[pl.BlockSpec((None, tm, D_MODEL), lambda b, i: (b, i, 0))] * 2,
        out_shape=[jax.ShapeDtypeStruct(x.shape, F32)] * 2,
        compiler_params=_cparams(("parallel", "parallel")),
        name="out_proj",
    )(oa, ob, o_f, o_b, proj, x, mod, out_gain, w_out)


ROUTER_TM = 512


def _router_kernel(x_ref, mod_ref, g_ref, rw_ref, aff_ref):
    h = _ada_norm_rows(x_ref[...], g_ref[...], mod_ref[3:4, :], mod_ref[4:5, :])
    h_hi = h.astype(BF16)
    h_lo = (h - h_hi.astype(F32)).astype(BF16)
    both = jnp.dot(h_hi, rw_ref[...], preferred_element_type=F32)
    logits = (both[:, :LANES] + both[:, LANES:]
              + jnp.dot(h_lo, rw_ref[:, :LANES], preferred_element_type=F32))
    lane = lax.broadcasted_iota(jnp.int32, logits.shape, 1)
    logits = jnp.where(lane < N_EXPERTS, logits, NEG_INF)
    p = jnp.exp(logits - jnp.max(logits, axis=-1, keepdims=True))
    aff_ref[...] = p / jnp.sum(p, axis=-1, keepdims=True)


def _router(x, mod, g_ffn, router_w, *, per_batch_mod):
    B, N, _ = x.shape
    tm = min(ROUTER_TM, N)
    mod_idx = (lambda b, i: (b, 0, 0)) if per_batch_mod else (lambda b, i: (0, 0, 0))
    return pl.pallas_call(
        _router_kernel,
        grid=(B, N // tm),
        in_specs=[pl.BlockSpec((None, tm, D_MODEL), lambda b, i: (b, i, 0)),
                  pl.BlockSpec((None, 8, D_MODEL), mod_idx),
                  pl.BlockSpec((1, D_MODEL), lambda b, i: (0, 0)),
                  pl.BlockSpec((D_MODEL, 2 * LANES), lambda b, i: (0, 0))],
        out_specs=pl.BlockSpec((None, tm, LANES), lambda b, i: (b, i, 0)),
        out_shape=jax.ShapeDtypeStruct((B, N, LANES), F32),
        compiler_params=_cparams(("parallel", "parallel")),
        name="moe_router",
    )(x, mod, g_ffn, router_w)


def _select_kernel(aff_ref, out_ref, afft_ref, slot_ref, vals_ref, *, n_tok, cap):
    n_blk = n_tok // LANES
    for t in range(n_blk):
        afft_ref[:, t * LANES:(t + 1) * LANES] = aff_ref[t * LANES:(t + 1) * LANES, :].T[:N_EXPERTS, :]
    afft = afft_ref[...]
    count = lambda m: jnp.sum(jnp.where(m, 1.0, 0.0), axis=1, keepdims=True)

    def thr_step(i, t):
        cand = t | (jnp.int32(1) << (30 - i))
        return jnp.where(count(afft >= pltpu.bitcast(cand, F32)) >= cap, cand, t)
    thr = pltpu.bitcast(lax.fori_loop(0, 31, thr_step, jnp.zeros((N_EXPERTS, 1), jnp.int32)), F32)
    above = afft > thr
    tie = afft == thr
    need = cap - count(above)
    tok = lax.broadcasted_iota(jnp.int32, (N_EXPERTS, n_tok), 1)
    n_bits = n_tok.bit_length()

    def cut_step(i, cut):
        cand = cut + (jnp.int32(1) << (n_bits - 1 - i))
        ok = jnp.logical_and(cand <= n_tok, count(jnp.logical_and(tie, tok < cand)) <= need)
        return jnp.where(ok, cand, cut)
    cut = lax.fori_loop(0, n_bits, cut_step, jnp.zeros((N_EXPERTS, 1), jnp.int32))
    sel = jnp.logical_or(above, jnp.logical_and(tie, tok < cut))

    r_i = lax.broadcasted_iota(jnp.int32, (LANES, LANES), 0)
    c_i = lax.broadcasted_iota(jnp.int32, (LANES, LANES), 1)
    tri = jnp.where(r_i <= c_i, 1.0, 0.0).astype(BF16)
    base = jnp.zeros((N_EXPERTS, 1), F32)
    for t in range(n_blk):
        s_blk = sel[:, t * LANES:(t + 1) * LANES]
        ones = jnp.where(s_blk, 1.0, 0.0)
        incl = jnp.dot(ones.astype(BF16), tri, preferred_element_type=F32)
        slot_ref[:, t * LANES:(t + 1) * LANES] = jnp.where(s_blk, base + incl - 1.0, -1.0)
        base = base + incl[:, LANES - 1:LANES]

    a = aff_ref[...]
    hi = a.astype(BF16).astype(F32)
    mid = (a - hi).astype(BF16).astype(F32)
    lo = a - hi - mid
    lane_v = lax.broadcasted_iota(jnp.int32, (n_tok, LANES), 1)
    tok_v = lax.broadcasted_iota(jnp.int32, (n_tok, LANES), 0)
    vals = hi + pltpu.roll(mid, N_EXPERTS, 1) + pltpu.roll(lo, 2 * N_EXPERTS, 1)
    vals = jnp.where(lane_v == 3 * N_EXPERTS, (tok_v // 64).astype(F32), vals)
    vals = jnp.where(lane_v == 3 * N_EXPERTS + 1, (tok_v % 64).astype(F32), vals)
    vals_ref[...] = vals.astype(BF16)

    chunk = min(cap, LANES)
    lane_o = lax.broadcasted_iota(jnp.int32, (chunk, LANES), 1)
    slot_id = lax.broadcasted_iota(jnp.int32, (chunk, 1), 0).astype(F32)

    def expert(e, carry):
        slot_e = slot_ref[pl.ds(e, 1), :]
        for c0 in range(0, cap, chunk):
            onehot = jnp.where(slot_e == slot_id + float(c0), 1.0, 0.0).astype(BF16)
            res = jnp.dot(onehot, vals_ref[...], preferred_element_type=F32)
            mine = jnp.logical_and(lane_o % N_EXPERTS == e, lane_o < 3 * N_EXPERTS)
            gate_c = jnp.sum(jnp.where(mine, res, 0.0), axis=1, keepdims=True)
            idx_c = jnp.sum(jnp.where(lane_o == 3 * N_EXPERTS, res * 64.0,
                                      jnp.where(lane_o == 3 * N_EXPERTS + 1, res, 0.0)), axis=1, keepdims=True)
            out_ref[e, c0:c0 + chunk, :] = jnp.where(lane_o == 0, idx_c, jnp.where(lane_o == 1, gate_c, 0.0))
        return carry

    lax.fori_loop(0, N_EXPERTS, expert, 0)


def _select(aff, cap):
    S, n_tok, _ = aff.shape
    return pl.pallas_call(
        functools.partial(_select_kernel, n_tok=n_tok, cap=cap),
        grid=(S,),
        in_specs=[pl.BlockSpec((None, n_tok, LANES), lambda s: (s, 0, 0))],
        out_specs=pl.BlockSpec((None, N_EXPERTS, cap, LANES), lambda s: (s, 0, 0, 0)),
        out_shape=jax.ShapeDtypeStruct((S, N_EXPERTS, cap, LANES), F32),
        scratch_shapes=[pltpu.VMEM((N_EXPERTS, n_tok), F32), pltpu.VMEM((N_EXPERTS, n_tok), F32),
                        pltpu.VMEM((n_tok, LANES), BF16)],
        compiler_params=_cparams(("parallel",)),
        name="moe_select",
    )(aff)


FFN_TF = 256


def _moe_ffn_kernel(*refs, shapes, n_f):
    n = len(shapes)
    per_stream = [refs[3 * i:3 * i + 3] for i in range(n)]
    g_ref, wg_ref, wu_ref, wd_ref = refs[3 * n:3 * n + 4]
    x_hbms = refs[3 * n + 4:5 * n + 4:2]
    acc_hbms = refs[5 * n + 4:6 * n + 4]
    stage = refs[6 * n + 4]
    bufs = [refs[6 * n + 5 + 2 * i:6 * n + 7 + 2 * i] for i in range(n)]
    sem = refs[-1]
    f = pl.program_id(1)
    sets = [(i, s) for i, (n_sets, _) in enumerate(shapes) for s in range(n_sets)]

    def row_copies(i, s, src, dst, scatter):
        idx_ref, rows = per_stream[i][0], shapes[i][1]

        def issue(r, c):
            t = idx_ref[s, 0, r]
            if scatter:
                pltpu.make_async_copy(src.at[pl.ds(r, 1), :], dst.at[pl.ds(t, 1), :], sem).start()
            else:
                pltpu.make_async_copy(src.at[pl.ds(t, 1), :], dst.at[pl.ds(r, 1), :], sem).start()
            return c
        lax.fori_loop(0, rows, issue, 0, unroll=8)
        whole = stage.at[pl.ds(0, rows), :]
        pltpu.make_async_copy(whole, whole, sem).wait()

    @pl.when(f == 0)
    def _():
        for i, s in sets:
            mod_ref, rows = per_stream[i][2], shapes[i][1]
            row_copies(i, s, x_hbms[i], stage, False)
            h = _ada_norm_rows(stage[0:rows, :], g_ref[...], mod_ref[s, 3:4, :], mod_ref[s, 4:5, :])
            bufs[i][0][s] = h.astype(BF16)

    wg = wg_ref[...].astype(BF16)
    wu = wu_ref[...].astype(BF16)
    wd = wd_ref[...].astype(BF16)
    for i, s in sets:
        hbuf, ybuf = bufs[i]
        hb = hbuf[s]
        a = jnp.dot(hb, wg, preferred_element_type=F32)
        u = jnp.dot(hb, wu, preferred_element_type=F32)
        y = jnp.dot((_silu(a) * u).astype(BF16), wd, preferred_element_type=F32)

        @pl.when(f == 0)
        def _():
            ybuf[s] = y

        @pl.when(f > 0)
        def _():
            ybuf[s] = ybuf[s] + y

    @pl.when(f == n_f - 1)
    def _():
        for i, s in sets:
            (_, gate_ref, mod_ref), rows = per_stream[i], shapes[i][1]
            row_copies(i, s, acc_hbms[i], stage, False)
            stage[0:rows, :] = stage[0:rows, :] + (mod_ref[s, 5:6, :] * gate_ref[s]) * bufs[i][1][s]
            row_copies(i, s, stage, acc_hbms[i], True)


def _moe_ffn(streams, g_ffn, w_gate, w_up, w_down, layer):
    n = len(streams)
    n_f = D_EXPERT // FFN_TF
    shapes = tuple((s[0].shape[1], s[0].shape[3]) for s in streams)
    in_specs, args = [], []
    for (idx, gate, mod, _, _), (S, rows) in zip(streams, shapes):
        in_specs += [pl.BlockSpec((None, S, 1, rows), lambda e, f: (e, 0, 0, 0), memory_space=pltpu.SMEM),
                     pl.BlockSpec((None, S, rows, 1), lambda e, f: (e, 0, 0, 0)),
                     pl.BlockSpec((S, 8, D_MODEL), lambda e, f: (0, 0, 0))]
        args += [idx, gate, mod]
    in_specs += [pl.BlockSpec((1, D_MODEL), lambda e, f: (0, 0)),
                 pl.BlockSpec((None, None, D_MODEL, FFN_TF), lambda e, f: (layer, e, 0, f)),
                 pl.BlockSpec((None, None, D_MODEL, FFN_TF), lambda e, f: (layer, e, 0, f)),
                 pl.BlockSpec((None, None, FFN_TF, D_MODEL), lambda e, f: (layer, e, f, 0))]
    args += [g_ffn, w_gate, w_up, w_down]
    for _, _, _, x2d, acc2d in streams:
        in_specs += [pl.BlockSpec(memory_space=pl.ANY)] * 2
        args += [x2d, acc2d]
    scratch = [pltpu.VMEM((max(r for _, r in shapes), D_MODEL), F32)]
    for S, rows in shapes:
        scratch += [pltpu.VMEM((S, rows, D_MODEL), BF16), pltpu.VMEM((S, rows, D_MODEL), F32)]
    return pl.pallas_call(
        functools.partial(_moe_ffn_kernel, shapes=shapes, n_f=n_f),
        grid=(N_EXPERTS, n_f),
        in_specs=in_specs,
        out_specs=[pl.BlockSpec(memory_space=pl.ANY)] * n,
        out_shape=[jax.ShapeDtypeStruct(s[4].shape, F32) for s in streams],
        scratch_shapes=scratch + [pltpu.SemaphoreType.DMA(())],
        input_output_aliases={3 * n + 4 + 2 * i + 1: i for i in range(n)},
        compiler_params=_cparams(("arbitrary", "arbitrary")),
        name="moe_ffn",
    )(*args)


def _route(x, mod, g_ffn, router_w, *, per_batch_mod):
    B, N, _ = x.shape
    cap = (EC_FACTOR * N) // N_EXPERTS
    aff = _router(x, mod, g_ffn, router_w, per_batch_mod=per_batch_mod)
    chosen = _select(aff, cap)
    rows_of = jnp.arange(B, dtype=jnp.int32)[:, None, None] * N
    idx = chosen[..., 0].astype(jnp.int32) + rows_of
    gate = chosen[..., 1]
    if per_batch_mod:
        idx = jnp.swapaxes(idx, 0, 1)[:, :, None, :]
        gate = jnp.swapaxes(gate, 0, 1)[..., None]
    else:
        idx = jnp.swapaxes(idx, 0, 1).reshape(N_EXPERTS, 1, 1, B * cap)
        gate = jnp.swapaxes(gate, 0, 1).reshape(N_EXPERTS, 1, B * cap, 1)
    return idx, gate


def _expert_choice(pairs, mods, g_ffn, router_w, w_gate, w_up, w_down, layer):
    streams = []
    for i, ((x, x_acc), mod) in enumerate(zip(pairs, mods)):
        idx, gate = _route(x, mod, g_ffn, router_w, per_batch_mod=(i == 0))
        flat = lambda a: a.reshape(-1, D_MODEL)
        streams.append((idx, gate, mod, flat(x), flat(x_acc)))
    outs = _moe_ffn(streams, g_ffn, w_gate, w_up, w_down, layer)
    return [o.reshape(p[0].shape) for o, p in zip(outs, pairs)]


def _rope_tables(n_tok):
    t = jnp.arange(n_tok, dtype=jnp.int32)
    pos = jnp.stack([t // GRID_W, t % GRID_W], axis=-1).astype(F32)
    n_freq = HEAD_DIM // 4
    freqs = ROPE_THETA ** (-jnp.arange(n_freq, dtype=F32) / n_freq)
    ang = pos[:, :, None] * freqs
    cos, sin = jnp.cos(ang), jnp.sin(ang)
    cos_t = jnp.concatenate([cos[:, 0], cos[:, 0], cos[:, 1], cos[:, 1]], axis=-1)
    sin_t = jnp.concatenate([-sin[:, 0], sin[:, 0], -sin[:, 1], sin[:, 1]], axis=-1)
    return cos_t, sin_t


def _pad_rows(a, rows=8):
    return jnp.pad(a, ((0, rows - a.shape[0]), (0, 0)))


def _split_bf16(w):
    hi = w.astype(BF16)
    return jnp.concatenate([hi, (w - hi.astype(F32)).astype(BF16)], axis=-1)


def _layer_weights(p):
    w_in = p["w_in"]
    gate_lanes = jnp.concatenate([jnp.zeros((2, 2 * DN_HEADS), F32),
                                  jnp.stack([p["dn_a_log"].reshape(-1), p["dn_dt_bias"].reshape(-1)])], axis=1)
    return {
        "g_mix": p["norm_mix"].reshape(1, D_MODEL),
        "g_ffn": p["norm_ffn"].reshape(1, D_MODEL),
        "w_main": w_in[:, :D_MAIN].astype(BF16),
        "w_small": jnp.pad(w_in[:, D_MAIN:], ((0, 0), (0, LANES - 4 * DN_HEADS))).astype(BF16),
        "gains": _pad_rows(jnp.concatenate([p["na_qk_gain"], p["gqa_qk_gain"]], axis=0)),
        "gate_params": _pad_rows(jnp.pad(gate_lanes, ((0, 0), (0, LANES - 4 * DN_HEADS)))),
        "w_out": p["w_out"].astype(BF16),
        "conv_w": _pad_rows(p["dn_conv"]),
        "out_gain": _pad_rows(p["dn_out_gain"].reshape(1, HEAD_DIM)),
        "router_w": _split_bf16(jnp.pad(p["router_w"], ((0, 0), (0, LANES - N_EXPERTS)))),
    }


def _hybrid_layer(x, ctx, mod_lat, mod_ctx, p, experts, na_bias, layer, tabs, need_ctx_out):
    W = _layer_weights(p)
    M = ctx.shape[1]
    proj = functools.partial(_project, g_mix=W["g_mix"], w_main=W["w_main"], w_small=W["w_small"],
                             gains=W["gains"], gate_params=W["gate_params"])
    p_lat, gates_lat = proj(x, mod_lat, cos_t=tabs[0], sin_t=tabs[1], rope=True, tm=1024, per_batch_mod=True)
    p_ctx, gates_ctx = proj(ctx, mod_ctx, cos_t=tabs[0][:M], sin_t=tabs[1][:M], rope=False, tm=M,
                            per_batch_mod=False)
    oa = _na_attention(p_lat, p_ctx, na_bias, layer)
    ob = _gqa_attention(p_lat, p_ctx)
    o_f, o_b = _deltanet(p_ctx, p_lat, gates_ctx, gates_lat, W["conv_w"])
    moe = functools.partial(_expert_choice, g_ffn=W["g_ffn"], router_w=W["router_w"], w_gate=experts[0],
                            w_up=experts[1], w_down=experts[2], layer=layer)
    x_pair = _out_project(oa, ob, o_f, o_b, p_lat, x, mod_lat, W["out_gain"], W["w_out"],
                          dn_row_off=M, per_batch_mod=True)
    if not need_ctx_out:
        return moe([x_pair], [mod_lat])[0], None
    oa_c, ob_c = _ctx_attention(p_ctx)
    ctx_pair = _out_project(oa_c, ob_c, o_f, o_b, p_ctx, ctx, mod_ctx, W["out_gain"], W["w_out"],
                            dn_row_off=0, per_batch_mod=False)
    x, ctx = moe([x_pair, ctx_pair], [mod_lat, mod_ctx])
    return x, ctx


def kernel(x, c, ctx, c_ctx, ada_w, ada_b, norm_mix, norm_ffn, w_in, w_out, na_qk_gain, na_rpb, gqa_qk_gain, dn_conv, dn_a_log, dn_dt_bias, dn_out_gain, router_w, exp_w_gate, exp_w_up, exp_w_down):
    B = x.shape[0]
    depth = ada_w.shape[0]
    c_rows = _pad_rows(jnp.concatenate([c, c_ctx[None, :]], axis=0))
    mods = _modulation(c_rows, ada_w, ada_b.reshape(depth, 1, -1)).reshape(depth, 8, N_MOD, D_MODEL)
    mods = jnp.pad(mods, ((0, 0), (0, 0), (0, 8 - N_MOD), (0, 0)))
    tabs = _rope_tables(x.shape[1])
    params = dict(norm_mix=norm_mix, norm_ffn=norm_ffn, w_in=w_in, w_out=w_out, na_qk_gain=na_qk_gain,
                  gqa_qk_gain=gqa_qk_gain, dn_conv=dn_conv, dn_a_log=dn_a_log,
                  dn_dt_bias=dn_dt_bias, dn_out_gain=dn_out_gain, router_w=router_w)
    experts = (exp_w_gate, exp_w_up, exp_w_down)
    na_bias = _na_bias_tables(na_rpb, x.shape[1] // GRID_W)
    for l in range(depth):
        p = {k: v[l] for k, v in params.items()}
        x, ctx = _hybrid_layer(x, ctx, mods[l, :B], mods[l, B:B + 1], p, experts, na_bias, l, tabs,
                               need_ctx_out=l < depth - 1)
    return x
```

```python
import functools
import math

import jax
import jax.numpy as jnp
import numpy as np
from jax import lax
from jax.experimental import pallas as pl
from jax.experimental.pallas import tpu as pltpu

D_MODEL = 2048
SEQ = 4096
DEPTH = 4
GRID_W = 64
CTX_LEN = 256
HEAD_DIM = 128
NA_HEADS = 4
NA_WIN_R = 8
NA_WIN_C = 16
GQA_Q_HEADS = 8
GQA_KV_HEADS = 2
ROPE_THETA = 10000.0
DN_HEADS = 4
DN_CONV = 5
W_A = NA_HEADS * HEAD_DIM
W_B = GQA_Q_HEADS * HEAD_DIM
W_KV = GQA_KV_HEADS * HEAD_DIM
W_C = DN_HEADS * HEAD_DIM
MIX_WIDTH = W_A + W_B + W_C
D_MAIN = 3 * W_A + W_B + 2 * W_KV + 4 * W_C
N_EXPERTS = 16
EC_FACTOR = 2
D_EXPERT = 1024
N_MOD = 6
NORM_EPS = 1e-6
NEG_INF = -1e30
F32 = jnp.float32
BF16 = jnp.bfloat16

LANES = 128
VMEM_LIMIT = 56 * 1024 * 1024
LOG2E = 1.4426950408889634


def _cparams(sem):
    return pltpu.CompilerParams(dimension_semantics=sem, vmem_limit_bytes=VMEM_LIMIT)


def _silu(x):
    return x * (1.0 / (1.0 + jnp.exp(-x)))


MOD_TN = 1024


def _mod_kernel(c_ref, w_ref, b_ref, o_ref):
    c = _silu(c_ref[...]).astype(BF16)
    o_ref[...] = jnp.dot(c, w_ref[...].astype(BF16), preferred_element_type=F32) + b_ref[...]


def _modulation(c_rows, ada_w, ada_b):
    n_out = N_MOD * D_MODEL
    depth = ada_w.shape[0]
    return pl.pallas_call(
        _mod_kernel,
        grid=(depth, n_out // MOD_TN),
        in_specs=[pl.BlockSpec((8, D_MODEL), lambda l, j: (0, 0)),
                  pl.BlockSpec((None, D_MODEL, MOD_TN), lambda l, j: (l, 0, j)),
                  pl.BlockSpec((None, 1, MOD_TN), lambda l, j: (l, 0, j))],
        out_specs=pl.BlockSpec((None, 8, MOD_TN), lambda l, j: (l, 0, j)),
        out_shape=jax.ShapeDtypeStruct((depth, 8, n_out), F32),
        compiler_params=_cparams(("parallel", "parallel")),
        name="adaln_mod",
    )(c_rows, ada_w, ada_b)


PROJ_TN = 512
N_PROJ_TILES = D_MAIN // PROJ_TN


def _ada_norm_rows(x, gain, shift, scale):
    y = x * lax.rsqrt(jnp.mean(x * x, axis=-1, keepdims=True) + NORM_EPS)
    return (y * gain) * (1.0 + scale) + shift


def _head_norm(a, gain):
    return a * lax.rsqrt(jnp.mean(a * a, axis=-1, keepdims=True) + NORM_EPS) * gain


def _rope(a, cos, sin):
    lane = lax.broadcasted_iota(jnp.int32, a.shape, 1)
    partner = jnp.where((lane % 64) < 32, pltpu.roll(a, LANES - 32, 1), pltpu.roll(a, 32, 1))
    return a * cos + partner * sin


def _proj_kernel(x_ref, mod_ref, g_ref, w_ref, ws_ref, gains_ref, gp_ref, cos_ref, sin_ref,
                 o_ref, gates_ref, h_ref, *, rope, tm):
    j = pl.program_id(2)

    @pl.when(j == 0)
    def _():
        h = _ada_norm_rows(x_ref[...], g_ref[...], mod_ref[0:1, :], mod_ref[1:2, :])
        hb = h.astype(BF16)
        h_ref[...] = hb
        raw = jnp.dot(hb, ws_ref[...], preferred_element_type=F32)
        lane = lax.broadcasted_iota(jnp.int32, raw.shape, 1)
        beta = 1.0 / (1.0 + jnp.exp(-raw))
        z = raw + gp_ref[1:2, :]
        softplus = jnp.maximum(z, 0.0) + jnp.log1p(jnp.exp(-jnp.abs(z)))
        gates_ref[...] = jnp.where(lane < 2 * DN_HEADS, beta, -jnp.exp(gp_ref[0:1, :]) * softplus)

    acc = jnp.dot(h_ref[...], w_ref[...], preferred_element_type=F32)

    def normed(col, gi, with_rope):
        a = _head_norm(acc[:, col * LANES:(col + 1) * LANES], gains_ref[gi:gi + 1, :])
        if with_rope and rope:
            a = _rope(a, cos_ref[...], sin_ref[...])
        return a.astype(BF16)

    @pl.when(j < 2)
    def _():
        gain = gains_ref[pl.ds(j, 1), :]
        for col in range(4):
            o_ref[:, col * LANES:(col + 1) * LANES] = _head_norm(
                acc[:, col * LANES:(col + 1) * LANES], gain).astype(BF16)

    @pl.when(jnp.logical_or(j == 3, j == 4))
    def _():
        for col in range(4):
            o_ref[:, col * LANES:(col + 1) * LANES] = normed(col, 2, True)

    @pl.when(j == 5)
    def _():
        for col in range(2):
            o_ref[:, col * LANES:(col + 1) * LANES] = normed(col, 3, True)
        o_ref[:, 2 * LANES:] = acc[:, 2 * LANES:].astype(BF16)

    @pl.when(jnp.logical_or(j == 2, j > 5))
    def _():
        o_ref[...] = acc.astype(BF16)


def _project(x, mod, g_mix, w_main, w_small, gains, gate_params, cos_t, sin_t, *, rope, tm, per_batch_mod):
    B, N, _ = x.shape
    mod_idx = (lambda b, i, j: (b, 0, 0)) if per_batch_mod else (lambda b, i, j: (0, 0, 0))
    return pl.pallas_call(
        functools.partial(_proj_kernel, rope=rope, tm=tm),
        grid=(B, N // tm, N_PROJ_TILES),
        in_specs=[pl.BlockSpec((None, tm, D_MODEL), lambda b, i, j: (b, i, 0)),
                  pl.BlockSpec((None, 8, D_MODEL), mod_idx),
                  pl.BlockSpec((1, D_MODEL), lambda b, i, j: (0, 0)),
                  pl.BlockSpec((D_MODEL, PROJ_TN), lambda b, i, j: (0, j)),
                  pl.BlockSpec((D_MODEL, LANES), lambda b, i, j: (0, 0)),
                  pl.BlockSpec((8, LANES), lambda b, i, j: (0, 0)),
                  pl.BlockSpec((8, LANES), lambda b, i, j: (0, 0)),
                  pl.BlockSpec((tm, LANES), lambda b, i, j: (i, 0)),
                  pl.BlockSpec((tm, LANES), lambda b, i, j: (i, 0))],
        out_specs=[pl.BlockSpec((None, tm, PROJ_TN), lambda b, i, j: (b, i, j)),
                   pl.BlockSpec((None, tm, LANES), lambda b, i, j: (b, i, 0))],
        out_shape=[jax.ShapeDtypeStruct((B, N, D_MAIN), BF16),
                   jax.ShapeDtypeStruct((B, N, LANES), F32)],
        scratch_shapes=[pltpu.VMEM((tm, D_MODEL), BF16)],
        compiler_params=_cparams(("parallel", "parallel", "arbitrary")),
        name="in_proj",
    )(x, mod, g_mix, w_main, w_small, gains, gate_params, cos_t, sin_t)


SM_SCALE = HEAD_DIM ** -0.5
_NT = (((1,), (1,)), ((), ()))
COL_QA, COL_KA, COL_VA = 0, W_A // LANES, 2 * W_A // LANES
COL_QB = 3 * W_A // LANES
COL_KB = COL_QB + W_B // LANES
COL_VB = COL_KB + W_KV // LANES
COL_QC = COL_VB + W_KV // LANES
COL_GC = COL_QC + 3 * W_C // LANES


def _softmax_pv(score_blocks, value_blocks):
    m = functools.reduce(jnp.maximum, [jnp.max(s, axis=-1, keepdims=True) for s in score_blocks])
    ps = [jnp.exp2((s - m) * (SM_SCALE * LOG2E)) for s in score_blocks]
    o = functools.reduce(jnp.add, [jnp.dot(p.astype(BF16), _with_ones(v), preferred_element_type=F32)
                                   for p, v in zip(ps, value_blocks)])
    return o[:, :LANES] / o[:, LANES:LANES + 1]


def _with_ones(v):
    ones = jnp.where(lax.broadcasted_iota(jnp.int32, v.shape, 1) == 0, 1.0, 0.0).astype(v.dtype)
    return jnp.concatenate([v, ones], axis=1)


GQA_TQ = 512
GQA_GROUP = GQA_Q_HEADS // GQA_KV_HEADS


GQA_TK = 2048


def _gqa_kernel(q_ref, kl_ref, vl_ref, kc_ref, vc_ref, o_ref):
    n_lat = kl_ref.shape[0]
    c_exp = SM_SCALE * LOG2E
    for r in range(GQA_GROUP):
        q = q_ref[:, r * LANES:(r + 1) * LANES]
        m = l = acc = None
        for c0 in [None] + list(range(0, n_lat, GQA_TK)):
            k = kc_ref[...] if c0 is None else kl_ref[c0:c0 + GQA_TK, :]
            v = vc_ref[...] if c0 is None else vl_ref[c0:c0 + GQA_TK, :]
            v_ext = _with_ones(v)
            s = lax.dot_general(q, k, _NT, preferred_element_type=F32)
            m_c = jnp.max(s, axis=-1, keepdims=True)
            if m is None:
                m = m_c
                p = jnp.exp2((s - m) * c_exp)
                acc = jnp.dot(p.astype(BF16), v_ext, preferred_element_type=F32)
            else:
                m_new = jnp.maximum(m, m_c)
                alpha = jnp.exp2((m - m_new) * c_exp)
                p = jnp.exp2((s - m_new) * c_exp)
                acc = alpha * acc + jnp.dot(p.astype(BF16), v_ext, preferred_element_type=F32)
                m = m_new
        o_ref[:, r * LANES:(r + 1) * LANES] = (acc[:, :LANES] / acc[:, LANES:LANES + 1]).astype(BF16)


def _gqa_attention(p_lat, p_ctx):
    B, N, _ = p_lat.shape
    M = p_ctx.shape[1]
    qw = GQA_GROUP * LANES
    return pl.pallas_call(
        _gqa_kernel,
        grid=(B, GQA_KV_HEADS, N // GQA_TQ),
        in_specs=[pl.BlockSpec((None, GQA_TQ, qw), lambda b, g, i: (b, i, COL_QB // GQA_GROUP + g)),
                  pl.BlockSpec((None, N, LANES), lambda b, g, i: (b, 0, COL_KB + g)),
                  pl.BlockSpec((None, N, LANES), lambda b, g, i: (b, 0, COL_VB + g)),
                  pl.BlockSpec((None, M, LANES), lambda b, g, i: (b, 0, COL_KB + g)),
                  pl.BlockSpec((None, M, LANES), lambda b, g, i: (b, 0, COL_VB + g))],
        out_specs=pl.BlockSpec((None, GQA_TQ, qw), lambda b, g, i: (b, i, g)),
        out_shape=jax.ShapeDtypeStruct((B, N, W_B), BF16),
        compiler_params=_cparams(("parallel", "parallel", "parallel")),
        name="gqa_attention",
    )(p_lat, p_lat, p_lat, p_ctx, p_ctx)


def _ctx_attn_kernel(p_ref, oa_ref, ob_ref):
    def col(c):
        return p_ref[:, c * LANES:(c + 1) * LANES]
    for h in range(NA_HEADS):
        s = lax.dot_general(col(COL_QA + h), col(COL_KA + h), _NT, preferred_element_type=F32)
        oa_ref[:, h * LANES:(h + 1) * LANES] = _softmax_pv([s], [col(COL_VA + h)]).astype(BF16)
    for h in range(GQA_Q_HEADS):
        g = h // GQA_GROUP
        s = lax.dot_general(col(COL_QB + h), col(COL_KB + g), _NT, preferred_element_type=F32)
        ob_ref[:, h * LANES:(h + 1) * LANES] = _softmax_pv([s], [col(COL_VB + g)]).astype(BF16)


def _ctx_attention(p_ctx):
    B, M, _ = p_ctx.shape
    return pl.pallas_call(
        _ctx_attn_kernel,
        grid=(B,),
        in_specs=[pl.BlockSpec((None, M, D_MAIN), lambda b: (b, 0, 0))],
        out_specs=[pl.BlockSpec((None, M, W_A), lambda b: (b, 0, 0)),
                   pl.BlockSpec((None, M, W_B), lambda b: (b, 0, 0))],
        out_shape=[jax.ShapeDtypeStruct((B, M, W_A), BF16), jax.ShapeDtypeStruct((B, M, W_B), BF16)],
        compiler_params=_cparams(("parallel",)),
        name="ctx_attention",
    )(p_ctx)


NA_ROWS = 8
NA_KROWS = 16
NA_TQ = NA_ROWS * GRID_W
NA_TK = NA_KROWS * GRID_W


NA_HPS = 4


def _na_kernel(q_ref, k_ref, v_ref, kc_ref, vc_ref, bias_ref, o_ref, *, n_rows):
    g = pl.program_id(2)
    kbase = jnp.clip(g * NA_ROWS - NA_WIN_R // 2, 0, n_rows - NA_KROWS)
    off = pl.multiple_of(kbase * GRID_W, GRID_W)
    for h in range(NA_HPS):
        lanes = slice(h * LANES, (h + 1) * LANES)
        q = q_ref[:, lanes]
        k = k_ref[pl.ds(off, NA_TK), lanes]
        v = v_ref[pl.ds(off, NA_TK), lanes]
        s_loc = lax.dot_general(q, k, _NT, preferred_element_type=F32) + bias_ref[h]
        s_ctx = lax.dot_general(q, kc_ref[:, lanes], _NT, preferred_element_type=F32)
        o_ref[:, lanes] = _softmax_pv([s_loc, s_ctx], [v, vc_ref[:, lanes]]).astype(BF16)


def _na_bias_tables(rpb, n_rows):
    wr = min(NA_WIN_R, n_rows)
    qc = np.arange(GRID_W)
    cstart = np.clip(qc - NA_WIN_C // 2, 0, GRID_W - NA_WIN_C)
    in_col = (qc[None, :] >= cstart[:, None]) & (qc[None, :] < cstart[:, None] + NA_WIN_C)
    dc = np.clip(qc[None, :] - qc[:, None] + (NA_WIN_C - 1), 0, 2 * NA_WIN_C - 2)
    pick = (dc[..., None] == np.arange(2 * NA_WIN_C - 1)).astype(np.float32)
    cols = jnp.einsum('lhab,qkb->lhaqk', rpb.astype(F32), pick, precision=lax.Precision.HIGHEST)
    cols = jnp.where(in_col, cols / SM_SCALE, NEG_INF)
    masked = jnp.full(cols.shape[:2] + (GRID_W, GRID_W), NEG_INF, F32)
    tabs = []
    for g in (0, 1, n_rows // NA_ROWS - 1):
        r0 = g * NA_ROWS
        kbase = min(max(r0 - NA_WIN_R // 2, 0), n_rows - NA_KROWS)
        rows = []
        for r in range(NA_ROWS):
            qr = r0 + r
            rstart = min(max(qr - wr // 2, 0), n_rows - wr)
            blocks = []
            for d in range(NA_KROWS):
                kr = kbase + d
                blocks.append(cols[:, :, kr - qr + NA_WIN_R - 1] if rstart <= kr < rstart + wr else masked)
            rows.append(jnp.concatenate(blocks, axis=-1))
        tabs.append(jnp.concatenate(rows, axis=-2))
    return jnp.stack(tabs, axis=2)


def _na_attention(p_lat, p_ctx, bias_tabs, layer):
    B, N, _ = p_lat.shape
    M = p_ctx.shape[1]
    n_rows = N // GRID_W
    n_groups = n_rows // NA_ROWS
    variant = lambda g: jnp.where(g == 0, 0, jnp.where(g == n_groups - 1, 2, 1))
    hw = NA_HPS * LANES
    return pl.pallas_call(
        functools.partial(_na_kernel, n_rows=n_rows),
        grid=(B, NA_HEADS // NA_HPS, n_groups),
        in_specs=[pl.BlockSpec((None, NA_TQ, hw), lambda b, h, g: (b, g, COL_QA // NA_HPS + h)),
                  pl.BlockSpec((None, N, hw), lambda b, h, g: (b, 0, COL_KA // NA_HPS + h)),
                  pl.BlockSpec((None, N, hw), lambda b, h, g: (b, 0, COL_VA // NA_HPS + h)),
                  pl.BlockSpec((None, M, hw), lambda b, h, g: (b, 0, COL_KA // NA_HPS + h)),
                  pl.BlockSpec((None, M, hw), lambda b, h, g: (b, 0, COL_VA // NA_HPS + h)),
                  pl.BlockSpec((None, NA_HPS, None, NA_TQ, NA_TK), lambda b, h, g: (layer, h, variant(g), 0, 0))],
        out_specs=pl.BlockSpec((None, NA_TQ, hw), lambda b, h, g: (b, g, h)),
        out_shape=jax.ShapeDtypeStruct((B, N, W_A), BF16),
        compiler_params=_cparams(("parallel", "parallel", "parallel")),
        name="na_attention",
    )(p_lat, p_lat, p_lat, p_ctx, p_ctx, bias_tabs)


DN_L = 128
DN_HALF = DN_L // 2
DN_CHAINS = 2 * DN_HEADS
_TN = (((0,), (0,)), ((), ()))


def _dn_conv_kernel(xc_ref, xl_ref, w_ref, o_ref, pad_ref, *, n_ctx, n_lat):
    which = pl.program_id(1)
    half = DN_CONV // 2
    q_scale = jnp.where(which == 0, HEAD_DIM ** -0.5, 1.0)

    def run(x_ref, n, out_off):
        pad_ref[0:8, :] = jnp.zeros((8, LANES), F32)
        pad_ref[8 + n:16 + n, :] = jnp.zeros((8, LANES), F32)
        pad_ref[8:8 + n, :] = x_ref[...].astype(F32)
        chunk = min(n, 256)
        for c0 in range(0, n, chunk):
            y = jnp.zeros((chunk, LANES), F32)
            for i in range(DN_CONV):
                y = y + w_ref[i:i + 1, :] * pad_ref[8 + c0 + i - half:8 + c0 + i - half + chunk, :]
            y = _silu(y)
            yn = y * lax.rsqrt(jnp.sum(y * y, axis=-1, keepdims=True) + NORM_EPS) * q_scale
            o_ref[out_off + c0:out_off + c0 + chunk, :] = jnp.where(which < 2, yn, y)

    run(xc_ref, n_ctx, 0)
    run(xl_ref, n_lat, n_ctx)


def _dn_conv(p_ctx, p_lat, conv_w):
    B, N, _ = p_lat.shape
    M = p_ctx.shape[1]
    col = lambda b, s, h: (b, 0, COL_QC + s * DN_HEADS + h)
    return pl.pallas_call(
        functools.partial(_dn_conv_kernel, n_ctx=M, n_lat=N),
        grid=(B, 3, DN_HEADS),
        in_specs=[pl.BlockSpec((None, M, LANES), col),
                  pl.BlockSpec((None, N, LANES), col),
                  pl.BlockSpec((8, LANES), lambda b, s, h: (0, s * DN_HEADS + h))],
        out_specs=pl.BlockSpec((None, None, M + N, LANES), lambda b, s, h: (b, s, 0, h)),
        out_shape=jax.ShapeDtypeStruct((B, 3, M + N, W_C), F32),
        scratch_shapes=[pltpu.VMEM((N + 16, LANES), F32)],
        compiler_params=_cparams(("parallel", "parallel", "parallel")),
        name="dn_conv",
    )(p_ctx, p_lat, conv_w)


def _scan_incl(x, axis):
    idx = lax.broadcasted_iota(jnp.int32, x.shape, axis)
    s = 1
    while s < DN_L:
        x = x + jnp.where(idx >= s, pltpu.roll(x, s, axis), 0.0)
        s *= 2
    return x


def _dn_gate_sums(gates, gates_t):
    pf = _scan_incl(gates, 0)
    tot = pf[DN_L - 1:DN_L, :]
    pf_t = _scan_incl(gates_t, 1)
    tot_t = pf_t[:, DN_L - 1:DN_L]
    return (pf, tot - pf + gates), (pf_t, tot_t - pf_t + gates_t), tot


def _dn_chain_terms(h, d, gates, gates_t, sums):
    (cum, cum_t, tot) = sums
    lb = d * DN_HEADS + h
    lg = 2 * DN_HEADS + lb
    row = lax.broadcasted_iota(jnp.int32, (DN_L, DN_L), 0)
    col = lax.broadcasted_iota(jnp.int32, (DN_L, DN_L), 1)
    return (gates[:, lb:lb + 1], gates_t[lb:lb + 1, :], cum[d][:, lg:lg + 1], cum_t[d][lg:lg + 1, :],
            tot[:, lg:lg + 1], row, col)


def _dn_lower_form(h, d, gram, gates, gates_t, sums):
    beta_c, beta_r, g_c, g_r, _, row, col = _dn_chain_terms(h, d, gates, gates_t, sums)
    if d == 0:
        m = beta_c * gram * jnp.exp(jnp.where(row > col, g_c - g_r, NEG_INF))
    else:
        m = beta_r * gram * jnp.exp(jnp.where(row > col, g_r - g_c, NEG_INF))
    return jnp.where(row > col, m, 0.0)


def _dn_gram_kernel(k_ref, g_ref, gt_ref, m_ref):
    gates, gates_t = g_ref[...], gt_ref[...]
    sums = _dn_gate_sums(gates, gates_t)
    for h in range(DN_HEADS):
        kb = k_ref[:, h * LANES:(h + 1) * LANES].astype(BF16)
        gram = lax.dot_general(kb, kb, _NT, preferred_element_type=F32)
        for d in range(2):
            m_ref[d * DN_HEADS + h] = _dn_lower_form(h, d, gram, gates, gates_t, sums)


def _dn_gram(dnx, gates, gates_t):
    B, _, T, _ = dnx.shape
    n_sc = T // DN_L
    return pl.pallas_call(
        _dn_gram_kernel,
        grid=(B, n_sc),
        in_specs=[pl.BlockSpec((None, None, DN_L, W_C), lambda b, s: (b, 1, s, 0)),
                  pl.BlockSpec((None, DN_L, LANES), lambda b, s: (b, s, 0)),
                  pl.BlockSpec((None, 16, DN_L), lambda b, s: (b, 0, s))],
        out_specs=pl.BlockSpec((None, None, DN_CHAINS, DN_L, DN_L), lambda b, s: (b, s, 0, 0, 0)),
        out_shape=jax.ShapeDtypeStruct((B, n_sc, DN_CHAINS, DN_L, DN_L), F32),
        compiler_params=_cparams(("parallel", "parallel")),
        name="dn_gram",
    )(dnx, gates, gates_t)


TRI_GROUP = LANES // 2


def _tri_inv_kernel(m_ref, t_ref, a_scr, t_scr):
    n = DN_HALF
    lane_n = lax.broadcasted_iota(jnp.int32, (n, LANES), 1)
    for i in range(n):
        rt = jnp.concatenate([m_ref[:, i, :], m_ref[:, n + i, :]], axis=0).T
        a_scr[i * n:(i + 1) * n, :] = jnp.where(lane_n < n, rt[:n], rt[n:])

    t_scr[...] = jnp.zeros(t_scr.shape, F32)
    col_id = lax.broadcasted_iota(jnp.int32, (n, LANES), 0)

    for ib in range(n // 8):
        def row_step(ii, carry, ib=ib):
            i = ib * 8 + ii
            acc = jnp.where(col_id == i, 1.0, 0.0)
            for jb in range(ib + 1):
                h = 8 * (jb + 1)
                part = acc[:h]
                for jj in range(8):
                    j = jb * 8 + jj
                    a = a_scr[pl.ds(i * n + j, 1), :]
                    part = part - a * t_scr[j * n:j * n + h, :]
                acc = part if h == n else jnp.concatenate([part, acc[h:]], axis=0)
            t_scr[pl.ds(pl.multiple_of(i * n, n), n), :] = acc
            return carry

        lax.fori_loop(0, 8, row_step, 0)

    for i in range(n):
        slab = t_scr[i * n:(i + 1) * n, :]
        st = jnp.concatenate([slab, slab], axis=0).T
        t_ref[:, i, :] = jnp.where(lane_n < n, st[:n], 0.0)
        t_ref[:, n + i, :] = jnp.where(lane_n >= n, st[n:], 0.0)


def _tri_inv(m_low):
    n = m_low.shape[0]
    assert n % TRI_GROUP == 0, n
    return pl.pallas_call(
        _tri_inv_kernel,
        grid=(n // TRI_GROUP,),
        in_specs=[pl.BlockSpec((TRI_GROUP, DN_L, DN_L), lambda g: (g, 0, 0))],
        out_specs=pl.BlockSpec((TRI_GROUP, DN_L, DN_L), lambda g: (g, 0, 0)),
        out_shape=jax.ShapeDtypeStruct(m_low.shape, F32),
        scratch_shapes=[pltpu.VMEM((DN_HALF * DN_HALF, LANES), F32), pltpu.VMEM((DN_HALF * DN_HALF, LANES), F32)],
        compiler_params=_cparams(("parallel",)),
        name="dn_tri_inv",
    )(m_low)


def _dn_prep_kernel(q_ref, k_ref, v_ref, g_ref, gt_ref, m_ref, t_ref, u_ref, ws_ref, wv_ref, gl_ref):
    gates, gates_t = g_ref[...], gt_ref[...]
    sums = _dn_gate_sums(gates, gates_t)
    H = DN_HALF
    for h in range(DN_HEADS):
        q = q_ref[:, h * LANES:(h + 1) * LANES]
        k = k_ref[:, h * LANES:(h + 1) * LANES]
        v = v_ref[:, h * LANES:(h + 1) * LANES]
        qk_raw = lax.dot_general(q.astype(BF16), k.astype(BF16), _NT, preferred_element_type=F32)
        for d in range(2):
            c = d * DN_HEADS + h
            beta_c, _, g_c, g_r, g_tot, row, col = _dn_chain_terms(h, d, gates, gates_t, sums)
            causal = (row >= col) if d == 0 else (row <= col)
            decay = jnp.exp(jnp.where(causal, g_c - g_r, NEG_INF))
            eg = jnp.exp(g_c)
            rhs = jnp.concatenate([v * beta_c, k * (beta_c * eg)], axis=1)
            t_bd = t_ref[c].astype(BF16)
            m21 = jnp.where(jnp.logical_and(row >= H, col < H), m_ref[c], 0.0).astype(BF16)
            apply = (lambda a, b: jnp.dot(a, b.astype(BF16), preferred_element_type=F32)) if d == 0 else (
                lambda a, b: lax.dot_general(a, b.astype(BF16), _TN, preferred_element_type=F32))
            first = apply(t_bd, rhs)
            sol = apply(t_bd, rhs - apply(m21, first))
            u_ref[c] = sol[:, :LANES]
            ws_ref[c, :DN_L, :] = sol[:, LANES:].astype(BF16)
            ws_ref[c, DN_L:, :] = (q * eg).astype(BF16)
            wv_ref[c, :DN_L, :] = (qk_raw * decay).astype(BF16)
            wv_ref[c, DN_L:, :] = (k * jnp.exp(g_tot - g_c)).T.astype(BF16)
            gl_ref[c] = jnp.broadcast_to(jnp.exp(g_tot), (8, LANES))


def _dn_prep(dnx, gates, gates_t, m_low, t_inv):
    B, _, T, _ = dnx.shape
    n_sc = T // DN_L
    blk = pl.BlockSpec((None, None, DN_CHAINS, DN_L, DN_L), lambda b, s: (b, s, 0, 0, 0))
    blk2 = pl.BlockSpec((None, None, DN_CHAINS, 2 * DN_L, DN_L), lambda b, s: (b, s, 0, 0, 0))
    stacked = jax.ShapeDtypeStruct((B, n_sc, DN_CHAINS, 2 * DN_L, DN_L), BF16)
    return pl.pallas_call(
        _dn_prep_kernel,
        grid=(B, n_sc),
        in_specs=[pl.BlockSpec((None, None, DN_L, W_C), lambda b, s: (b, 0, s, 0)),
                  pl.BlockSpec((None, None, DN_L, W_C), lambda b, s: (b, 1, s, 0)),
                  pl.BlockSpec((None, None, DN_L, W_C), lambda b, s: (b, 2, s, 0)),
                  pl.BlockSpec((None, DN_L, LANES), lambda b, s: (b, s, 0)),
                  pl.BlockSpec((None, 16, DN_L), lambda b, s: (b, 0, s)),
                  blk, blk],
        out_specs=[blk, blk2, blk2,
                   pl.BlockSpec((None, None, DN_CHAINS, 8, LANES), lambda b, s: (b, s, 0, 0, 0))],
        out_shape=[jax.ShapeDtypeStruct((B, n_sc, DN_CHAINS, DN_L, DN_L), F32), stacked, stacked,
                   jax.ShapeDtypeStruct((B, n_sc, DN_CHAINS, 8, LANES), F32)],
        compiler_params=_cparams(("parallel", "parallel")),
        name="dn_prep",
    )(dnx, dnx, dnx, gates, gates_t, m_low, t_inv)


def _dn_scan_kernel(uf, wsf, wvf, glf, ub, wsb, wvb, glb, of_ref, ob_ref, s_ref):
    @pl.when(pl.program_id(1) == 0)
    def _():
        s_ref[...] = jnp.zeros(s_ref.shape, F32)

    for d, (u, ws, wv, gl, o_ref) in enumerate(((uf, wsf, wvf, glf, of_ref), (ub, wsb, wvb, glb, ob_ref))):
        for h in range(DN_HEADS):
            c = d * DN_HEADS + h
            s = s_ref[c]
            on_s = jnp.dot(ws[h], s.astype(BF16), preferred_element_type=F32)
            v_new = u[h] - on_s[:DN_L]
            on_v = jnp.dot(wv[h], v_new.astype(BF16), preferred_element_type=F32)
            s_ref[c] = s * gl[h][0:1, :] + on_v[DN_L:]
            o_ref[:, h * LANES:(h + 1) * LANES] = on_s[DN_L:] + on_v[:DN_L]


def _dn_scan(u, ws, wv, gl, n_ctx_chunks):
    B, n_sc = u.shape[0], u.shape[1]
    fwd = lambda b, s: s
    bwd = lambda b, s: jnp.where(s < n_ctx_chunks, n_ctx_chunks - 1 - s, n_sc - 1 + n_ctx_chunks - s)
    specs = []
    for d, walk in ((0, fwd), (1, bwd)):
        for rows in (DN_L, 2 * DN_L, 2 * DN_L, 8):
            specs.append(pl.BlockSpec((None, None, DN_HEADS, rows, LANES),
                                      functools.partial(lambda b, s, d, walk: (b, walk(b, s), d, 0, 0), d=d, walk=walk)))
    out = jax.ShapeDtypeStruct((B, n_sc * DN_L, W_C), F32)
    return pl.pallas_call(
        _dn_scan_kernel,
        grid=(B, n_sc),
        in_specs=specs,
        out_specs=[pl.BlockSpec((None, DN_L, W_C), lambda b, s: (b, fwd(b, s), 0)),
                   pl.BlockSpec((None, DN_L, W_C), lambda b, s: (b, bwd(b, s), 0))],
        out_shape=[out, out],
        scratch_shapes=[pltpu.VMEM((DN_CHAINS, DN_L, DN_L), F32)],
        compiler_params=_cparams(("parallel", "arbitrary")),
        name="dn_scan",
    )(u, ws, wv, gl, u, ws, wv, gl)


def _deltanet(p_ctx, p_lat, gates_ctx, gates_lat, conv_w):
    B = p_lat.shape[0]
    n_ctx_chunks = p_ctx.shape[1] // DN_L
    dnx = _dn_conv(p_ctx, p_lat, conv_w)
    gates = jnp.concatenate([gates_ctx, gates_lat], axis=1)
    gates_t = jnp.swapaxes(gates[:, :, :16], 1, 2)
    m_low = _dn_gram(dnx, gates, gates_t)
    t_inv = _tri_inv(m_low.reshape(-1, DN_L, DN_L)).reshape(m_low.shape)
    u, ws, wv, gl = _dn_prep(dnx, gates, gates_t, m_low, t_inv)
    return _dn_scan(u, ws, wv, gl, n_ctx_chunks)


OUT_TM = 256


def _out_proj_kernel(oa_ref, ob_ref, of_ref, obk_ref, gc_ref, x_ref, mod_ref, gain_ref, w_ref, o_ref, o2_ref):
    o_dn = of_ref[...] + obk_ref[...]
    acc = jnp.dot(oa_ref[...], w_ref[0:W_A, :], preferred_element_type=F32)
    acc = acc + jnp.dot(ob_ref[...], w_ref[W_A:W_A + W_B, :], preferred_element_type=F32)
    for h in range(DN_HEADS):
        oh = _head_norm(o_dn[:, h * LANES:(h + 1) * LANES], gain_ref[0:1, :])
        gate = gc_ref[:, h * LANES:(h + 1) * LANES].astype(F32)
        oc = (oh * _silu(gate)).astype(BF16)
        lo = W_A + W_B + h * LANES
        acc = acc + jnp.dot(oc, w_ref[lo:lo + LANES, :], preferred_element_type=F32)
    y = x_ref[...] + mod_ref[2:3, :] * acc
    o_ref[...] = y
    o2_ref[...] = y


def _out_project(oa, ob, o_f, o_b, proj, x, mod, out_gain, w_out, *, dn_row_off, per_batch_mod):
    B, N, _ = x.shape
    tm = OUT_TM
    off = dn_row_off // tm
    mod_idx = (lambda b, i: (b, 0, 0)) if per_batch_mod else (lambda b, i: (0, 0, 0))
    return pl.pallas_call(
        _out_proj_kernel,
        grid=(B, N // tm),
        in_specs=[pl.BlockSpec((None, tm, W_A), lambda b, i: (b, i, 0)),
                  pl.BlockSpec((None, tm, W_B), lambda b, i: (b, i, 0)),
                  pl.BlockSpec((None, tm, W_C), lambda b, i: (b, i + off, 0)),
                  pl.BlockSpec((None, tm, W_C), lambda b, i: (b, i + off, 0)),
                  pl.BlockSpec((None, tm, W_C), lambda b, i: (b, i, COL_GC * LANES // W_C)),
                  pl.BlockSpec((None, tm, D_MODEL), lambda b, i: (b, i, 0)),
                  pl.BlockSpec((None, 8, D_MODEL), mod_idx),
                  pl.BlockSpec((8, LANES), lambda b, i: (0, 0)),
                  pl.BlockSpec((MIX_WIDTH, D_MODEL), lambda b, i: (0, 0))],
        out_specs=[pl.BlockSpec((None, tm, D_MODEL), lambda b, i: (b, i, 0))] * 2,
        out_shape=[jax.ShapeDtypeStruct(x.shape, F32)] * 2,
        compiler_params=_cparams(("parallel", "parallel")),
        name="out_proj",
    )(oa, ob, o_f, o_b, proj, x, mod, out_gain, w_out)


ROUTER_TM = 512


def _router_kernel(x_ref, mod_ref, g_ref, rw_ref, aff_ref):
    h = _ada_norm_rows(x_ref[...], g_ref[...], mod_ref[3:4, :], mod_ref[4:5, :])
    h_hi = h.astype(BF16)
    h_lo = (h - h_hi.astype(F32)).astype(BF16)
    both = jnp.dot(h_hi, rw_ref[...], preferred_element_type=F32)
    logits = (both[:, :LANES] + both[:, LANES:]
              + jnp.dot(h_lo, rw_ref[:, :LANES], preferred_element_type=F32))
    lane = lax.broadcasted_iota(jnp.int32, logits.shape, 1)
    logits = jnp.where(lane < N_EXPERTS, logits, NEG_INF)
    p = jnp.exp(logits - jnp.max(logits, axis=-1, keepdims=True))
    aff_ref[...] = p / jnp.sum(p, axis=-1, keepdims=True)


def _router(x, mod, g_ffn, router_w, *, per_batch_mod):
    B, N, _ = x.shape
    tm = min(ROUTER_TM, N)
    mod_idx = (lambda b, i: (b, 0, 0)) if per_batch_mod else (lambda b, i: (0, 0, 0))
    return pl.pallas_call(
        _router_kernel,
        grid=(B, N // tm),
        in_specs=[pl.BlockSpec((None, tm, D_MODEL), lambda b, i: (b, i, 0)),
                  pl.BlockSpec((None, 8, D_MODEL), mod_idx),
                  pl.BlockSpec((1, D_MODEL), lambda b, i: (0, 0)),
                  pl.BlockSpec((D_MODEL, 2 * LANES), lambda b, i: (0, 0))],
        out_specs=pl.BlockSpec((None, tm, LANES), lambda b, i: (b, i, 0)),
        out_shape=jax.ShapeDtypeStruct((B, N, LANES), F32),
        compiler_params=_cparams(("parallel", "parallel")),
        name="moe_router",
    )(x, mod, g_ffn, router_w)


def _select_kernel(aff_ref, out_ref, afft_ref, slot_ref, vals_ref, *, n_tok, cap):
    n_blk = n_tok // LANES
    for t in range(n_blk):
        afft_ref[:, t * LANES:(t + 1) * LANES] = aff_ref[t * LANES:(t + 1) * LANES, :].T[:N_EXPERTS, :]
    afft = afft_ref[...]
    count = lambda m: jnp.sum(jnp.where(m, 1.0, 0.0), axis=1, keepdims=True)

    def thr_step(i, t):
        cand = t | (jnp.int32(1) << (30 - i))
        return jnp.where(count(afft >= pltpu.bitcast(cand, F32)) >= cap, cand, t)
    thr = pltpu.bitcast(lax.fori_loop(0, 31, thr_step, jnp.zeros((N_EXPERTS, 1), jnp.int32)), F32)
    above = afft > thr
    tie = afft == thr
    need = cap - count(above)
    tok = lax.broadcasted_iota(jnp.int32, (N_EXPERTS, n_tok), 1)
    n_bits = n_tok.bit_length()

    def cut_step(i, cut):
        cand = cut + (jnp.int32(1) << (n_bits - 1 - i))
        ok = jnp.logical_and(cand <= n_tok, count(jnp.logical_and(tie, tok < cand)) <= need)
        return jnp.where(ok, cand, cut)
    cut = lax.fori_loop(0, n_bits, cut_step, jnp.zeros((N_EXPERTS, 1), jnp.int32))
    sel = jnp.logical_or(above, jnp.logical_and(tie, tok < cut))

    r_i = lax.broadcasted_iota(jnp.int32, (LANES, LANES), 0)
    c_i = lax.broadcasted_iota(jnp.int32, (LANES, LANES), 1)
    tri = jnp.where(r_i <= c_i, 1.0, 0.0).astype(BF16)
    base = jnp.zeros((N_EXPERTS, 1), F32)
    for t in range(n_blk):
        s_blk = sel[:, t * LANES:(t + 1) * LANES]
        ones = jnp.where(s_blk, 1.0, 0.0)
        incl = jnp.dot(ones.astype(BF16), tri, preferred_element_type=F32)
        slot_ref[:, t * LANES:(t + 1) * LANES] = jnp.where(s_blk, base + incl - 1.0, -1.0)
        base = base + incl[:, LANES - 1:LANES]

    a = aff_ref[...]
    hi = a.astype(BF16).astype(F32)
    mid = (a - hi).astype(BF16).astype(F32)
    lo = a - hi - mid
    lane_v = lax.broadcasted_iota(jnp.int32, (n_tok, LANES), 1)
    tok_v = lax.broadcasted_iota(jnp.int32, (n_tok, LANES), 0)
    vals = hi + pltpu.roll(mid, N_EXPERTS, 1) + pltpu.roll(lo, 2 * N_EXPERTS, 1)
    vals = jnp.where(lane_v == 3 * N_EXPERTS, (tok_v // 64).astype(F32), vals)
    vals = jnp.where(lane_v == 3 * N_EXPERTS + 1, (tok_v % 64).astype(F32), vals)
    vals_ref[...] = vals.astype(BF16)

    chunk = min(cap, LANES)
    lane_o = lax.broadcasted_iota(jnp.int32, (chunk, LANES), 1)
    slot_id = lax.broadcasted_iota(jnp.int32, (chunk, 1), 0).astype(F32)

    def expert(e, carry):
        slot_e = slot_ref[pl.ds(e, 1), :]
        for c0 in range(0, cap, chunk):
            onehot = jnp.where(slot_e == slot_id + float(c0), 1.0, 0.0).astype(BF16)
            res = jnp.dot(onehot, vals_ref[...], preferred_element_type=F32)
            mine = jnp.logical_and(lane_o % N_EXPERTS == e, lane_o < 3 * N_EXPERTS)
            gate_c = jnp.sum(jnp.where(mine, res, 0.0), axis=1, keepdims=True)
            idx_c = jnp.sum(jnp.where(lane_o == 3 * N_EXPERTS, res * 64.0,
                                      jnp.where(lane_o == 3 * N_EXPERTS + 1, res, 0.0)), axis=1, keepdims=True)
            out_ref[e, c0:c0 + chunk, :] = jnp.where(lane_o == 0, idx_c, jnp.where(lane_o == 1, gate_c, 0.0))
        return carry

    lax.fori_loop(0, N_EXPERTS, expert, 0)


def _select(aff, cap):
    S, n_tok, _ = aff.shape
    return pl.pallas_call(
        functools.partial(_select_kernel, n_tok=n_tok, cap=cap),
        grid=(S,),
        in_specs=[pl.BlockSpec((None, n_tok, LANES), lambda s: (s, 0, 0))],
        out_specs=pl.BlockSpec((None, N_EXPERTS, cap, LANES), lambda s: (s, 0, 0, 0)),
        out_shape=jax.ShapeDtypeStruct((S, N_EXPERTS, cap, LANES), F32),
        scratch_shapes=[pltpu.VMEM((N_EXPERTS, n_tok), F32), pltpu.VMEM((N_EXPERTS, n_tok), F32),
                        pltpu.VMEM((n_tok, LANES), BF16)],
        compiler_params=_cparams(("parallel",)),
        name="moe_select",
    )(aff)


FFN_TF = 256


def _moe_ffn_kernel(*refs, shapes, n_f):
    n = len(shapes)
    per_stream = [refs[3 * i:3 * i + 3] for i in range(n)]
    g_ref, wg_ref, wu_ref, wd_ref = refs[3 * n:3 * n + 4]
    x_hbms = refs[3 * n + 4:5 * n + 4:2]
    acc_hbms = refs[5 * n + 4:6 * n + 4]
    stage = refs[6 * n + 4]
    bufs = [refs[6 * n + 5 + 2 * i:6 * n + 7 + 2 * i] for i in range(n)]
    sem = refs[-1]
    f = pl.program_id(1)
    sets = [(i, s) for i, (n_sets, _) in enumerate(shapes) for s in range(n_sets)]

    def row_copies(i, s, src, dst, scatter):
        idx_ref, rows = per_stream[i][0], shapes[i][1]

        def issue(r, c):
            t = idx_ref[s, 0, r]
            if scatter:
                pltpu.make_async_copy(src.at[pl.ds(r, 1), :], dst.at[pl.ds(t, 1), :], sem).start()
            else:
                pltpu.make_async_copy(src.at[pl.ds(t, 1), :], dst.at[pl.ds(r, 1), :], sem).start()
            return c
        lax.fori_loop(0, rows, issue, 0, unroll=8)
        whole = stage.at[pl.ds(0, rows), :]
        pltpu.make_async_copy(whole, whole, sem).wait()

    @pl.when(f == 0)
    def _():
        for i, s in sets:
            mod_ref, rows = per_stream[i][2], shapes[i][1]
            row_copies(i, s, x_hbms[i], stage, False)
            h = _ada_norm_rows(stage[0:rows, :], g_ref[...], mod_ref[s, 3:4, :], mod_ref[s, 4:5, :])
            bufs[i][0][s] = h.astype(BF16)

    wg = wg_ref[...].astype(BF16)
    wu = wu_ref[...].astype(BF16)
    wd = wd_ref[...].astype(BF16)
    for i, s in sets:
        hbuf, ybuf = bufs[i]
        hb = hbuf[s]
        a = jnp.dot(hb, wg, preferred_element_type=F32)
        u = jnp.dot(hb, wu, preferred_element_type=F32)
        y = jnp.dot((_silu(a) * u).astype(BF16), wd, preferred_element_type=F32)

        @pl.when(f == 0)
        def _():
            ybuf[s] = y

        @pl.when(f > 0)
        def _():
            ybuf[s] = ybuf[s] + y

    @pl.when(f == n_f - 1)
    def _():
        for i, s in sets:
            (_, gate_ref, mod_ref), rows = per_stream[i], shapes[i][1]
            row_copies(i, s, acc_hbms[i], stage, False)
            stage[0:rows, :] = stage[0:rows, :] + (mod_ref[s, 5:6, :] * gate_ref[s]) * bufs[i][1][s]
            row_copies(i, s, stage, acc_hbms[i], True)


def _moe_ffn(streams, g_ffn, w_gate, w_up, w_down, layer):
    n = len(streams)
    n_f = D_EXPERT // FFN_TF
    shapes = tuple((s[0].shape[1], s[0].shape[3]) for s in streams)
    in_specs, args = [], []
    for (idx, gate, mod, _, _), (S, rows) in zip(streams, shapes):
        in_specs += [pl.BlockSpec((None, S, 1, rows), lambda e, f: (e, 0, 0, 0), memory_space=pltpu.SMEM),
                     pl.BlockSpec((None, S, rows, 1), lambda e, f: (e, 0, 0, 0)),
                     pl.BlockSpec((S, 8, D_MODEL), lambda e, f: (0, 0, 0))]
        args += [idx, gate, mod]
    in_specs += [pl.BlockSpec((1, D_MODEL), lambda e, f: (0, 0)),
                 pl.BlockSpec((None, None, D_MODEL, FFN_TF), lambda e, f: (layer, e, 0, f)),
                 pl.BlockSpec((None, None, D_MODEL, FFN_TF), lambda e, f: (layer, e, 0, f)),
                 pl.BlockSpec((None, None, FFN_TF, D_MODEL), lambda e, f: (layer, e, f, 0))]
    args += [g_ffn, w_gate, w_up, w_down]
    for _, _, _, x2d, acc2d in streams:
        in_specs += [pl.BlockSpec(memory_space=pl.ANY)] * 2
        args += [x2d, acc2d]
    scratch = [pltpu.VMEM((max(r for _, r in shapes), D_MODEL), F32)]
    for S, rows in shapes:
        scratch += [pltpu.VMEM((S, rows, D_MODEL), BF16), pltpu.VMEM((S, rows, D_MODEL), F32)]
    return pl.pallas_call(
        functools.partial(_moe_ffn_kernel, shapes=shapes, n_f=n_f),
        grid=(N_EXPERTS, n_f),
        in_specs=in_specs,
        out_specs=[pl.BlockSpec(memory_space=pl.ANY)] * n,
        out_shape=[jax.ShapeDtypeStruct(s[4].shape, F32) for s in streams],
        scratch_shapes=scratch + [pltpu.SemaphoreType.DMA(())],
        input_output_aliases={3 * n + 4 + 2 * i + 1: i for i in range(n)},
        compiler_params=_cparams(("arbitrary", "arbitrary")),
        name="moe_ffn",
    )(*args)


def _route(x, mod, g_ffn, router_w, *, per_batch_mod):
    B, N, _ = x.shape
    cap = (EC_FACTOR * N) // N_EXPERTS
    aff = _router(x, mod, g_ffn, router_w, per_batch_mod=per_batch_mod)
    chosen = _select(aff, cap)
    rows_of = jnp.arange(B, dtype=jnp.int32)[:, None, None] * N
    idx = chosen[..., 0].astype(jnp.int32) + rows_of
    gate = chosen[..., 1]
    if per_batch_mod:
        idx = jnp.swapaxes(idx, 0, 1)[:, :, None, :]
        gate = jnp.swapaxes(gate, 0, 1)[..., None]
    else:
        idx = jnp.swapaxes(idx, 0, 1).reshape(N_EXPERTS, 1, 1, B * cap)
        gate = jnp.swapaxes(gate, 0, 1).reshape(N_EXPERTS, 1, B * cap, 1)
    return idx, gate


def _expert_choice(pairs, mods, g_ffn, router_w, w_gate, w_up, w_down, layer):
    streams = []
    for i, ((x, x_acc), mod) in enumerate(zip(pairs, mods)):
        idx, gate = _route(x, mod, g_ffn, router_w, per_batch_mod=(i == 0))
        flat = lambda a: a.reshape(-1, D_MODEL)
        streams.append((idx, gate, mod, flat(x), flat(x_acc)))
    outs = _moe_ffn(streams, g_ffn, w_gate, w_up, w_down, layer)
    return [o.reshape(p[0].shape) for o, p in zip(outs, pairs)]


def _rope_tables(n_tok):
    t = jnp.arange(n_tok, dtype=jnp.int32)
    pos = jnp.stack([t // GRID_W, t % GRID_W], axis=-1).astype(F32)
    n_freq = HEAD_DIM // 4
    freqs = ROPE_THETA ** (-jnp.arange(n_freq, dtype=F32) / n_freq)
    ang = pos[:, :, None] * freqs
    cos, sin = jnp.cos(ang), jnp.sin(ang)
    cos_t = jnp.concatenate([cos[:, 0], cos[:, 0], cos[:, 1], cos[:, 1]], axis=-1)
    sin_t = jnp.concatenate([-sin[:, 0], sin[:, 0], -sin[:, 1], sin[:, 1]], axis=-1)
    return cos_t, sin_t


def _pad_rows(a, rows=8):
    return jnp.pad(a, ((0, rows - a.shape[0]), (0, 0)))


def _split_bf16(w):
    hi = w.astype(BF16)
    return jnp.concatenate([hi, (w - hi.astype(F32)).astype(BF16)], axis=-1)


def _layer_weights(p):
    w_in = p["w_in"]
    gate_lanes = jnp.concatenate([jnp.zeros((2, 2 * DN_HEADS), F32),
                                  jnp.stack([p["dn_a_log"].reshape(-1), p["dn_dt_bias"].reshape(-1)])], axis=1)
    return {
        "g_mix": p["norm_mix"].reshape(1, D_MODEL),
        "g_ffn": p["norm_ffn"].reshape(1, D_MODEL),
        "w_main": w_in[:, :D_MAIN].astype(BF16),
        "w_small": jnp.pad(w_in[:, D_MAIN:], ((0, 0), (0, LANES - 4 * DN_HEADS))).astype(BF16),
        "gains": _pad_rows(jnp.concatenate([p["na_qk_gain"], p["gqa_qk_gain"]], axis=0)),
        "gate_params": _pad_rows(jnp.pad(gate_lanes, ((0, 0), (0, LANES - 4 * DN_HEADS)))),
        "w_out": p["w_out"].astype(BF16),
        "conv_w": _pad_rows(p["dn_conv"]),
        "out_gain": _pad_rows(p["dn_out_gain"].reshape(1, HEAD_DIM)),
        "router_w": _split_bf16(jnp.pad(p["router_w"], ((0, 0), (0, LANES - N_EXPERTS)))),
    }


def _hybrid_layer(x, ctx, mod_lat, mod_ctx, p, experts, na_bias, layer, tabs, need_ctx_out):
    W = _layer_weights(p)
    M = ctx.shape[1]
    proj = functools.partial(_project, g_mix=W["g_mix"], w_main=W["w_main"], w_small=W["w_small"],
                             gains=W["gains"], gate_params=W["gate_params"])
    p_lat, gates_lat = proj(x, mod_lat, cos_t=tabs[0], sin_t=tabs[1], rope=True, tm=1024, per_batch_mod=True)
    p_ctx, gates_ctx = proj(ctx, mod_ctx, cos_t=tabs[0][:M], sin_t=tabs[1][:M], rope=False, tm=M,
                            per_batch_mod=False)
    oa = _na_attention(p_lat, p_ctx, na_bias, layer)
    ob = _gqa_attention(p_lat, p_ctx)
    o_f, o_b = _deltanet(p_ctx, p_lat, gates_ctx, gates_lat, W["conv_w"])
    moe = functools.partial(_expert_choice, g_ffn=W["g_ffn"], router_w=W["router_w"], w_gate=experts[0],
                            w_up=experts[1], w_down=experts[2], layer=layer)
    x_pair = _out_project(oa, ob, o_f, o_b, p_lat, x, mod_lat, W["out_gain"], W["w_out"],
                          dn_row_off=M, per_batch_mod=True)
    if not need_ctx_out:
        return moe([x_pair], [mod_lat])[0], None
    oa_c, ob_c = _ctx_attention(p_ctx)
    ctx_pair = _out_project(oa_c, ob_c, o_f, o_b, p_ctx, ctx, mod_ctx, W["out_gain"], W["w_out"],
                            dn_row_off=0, per_batch_mod=False)
    x, ctx = moe([x_pair, ctx_pair], [mod_lat, mod_ctx])
    return x, ctx


def kernel(x, c, ctx, c_ctx, ada_w, ada_b, norm_mix, norm_ffn, w_in, w_out, na_qk_gain, na_rpb, gqa_qk_gain, dn_conv, dn_a_log, dn_dt_bias, dn_out_gain, router_w, exp_w_gate, exp_w_up, exp_w_down):
    B = x.shape[0]
    depth = ada_w.shape[0]
    c_rows = _pad_rows(jnp.concatenate([c, c_ctx[None, :]], axis=0))
    mods = _modulation(c_rows, ada_w, ada_b.reshape(depth, 1, -1)).reshape(depth, 8, N_MOD, D_MODEL)
    mods = jnp.pad(mods, ((0, 0), (0, 0), (0, 8 - N_MOD), (0, 0)))
    tabs = _rope_tables(x.shape[1])
    params = dict(norm_mix=norm_mix, norm_ffn=norm_ffn, w_in=w_in, w_out=w_out, na_qk_gain=na_qk_gain,
                  gqa_qk_gain=gqa_qk_gain, dn_conv=dn_conv, dn_a_log=dn_a_log,
                  dn_dt_bias=dn_dt_bias, dn_out_gain=dn_out_gain, router_w=router_w)
    experts = (exp_w_gate, exp_w_up, exp_w_down)
    na_bias = _na_bias_tables(na_rpb, x.shape[1] // GRID_W)
    for l in range(depth):
        p = {k: v[l] for k, v in params.items()}
        x, ctx = _hybrid_layer(x, ctx, mods[l, :B], mods[l, B:B + 1], p, experts, na_bias, l, tabs,
                               need_ctx_out=l < depth - 1)
    return x
```

```python
import functools
import math

import jax
import jax.numpy as jnp
import numpy as np
from jax import lax
from jax.experimental import pallas as pl
from jax.experimental.pallas import tpu as pltpu

D_MODEL = 2048
SEQ = 4096
DEPTH = 4
GRID_W = 64
CTX_LEN = 256
HEAD_DIM = 128
NA_HEADS = 4
NA_WIN_R = 8
NA_WIN_C = 16
GQA_Q_HEADS = 8
GQA_KV_HEADS = 2
ROPE_THETA = 10000.0
DN_HEADS = 4
DN_CONV = 5
W_A = NA_HEADS * HEAD_DIM
W_B = GQA_Q_HEADS * HEAD_DIM
W_KV = GQA_KV_HEADS * HEAD_DIM
W_C = DN_HEADS * HEAD_DIM
MIX_WIDTH = W_A + W_B + W_C
D_MAIN = 3 * W_A + W_B + 2 * W_KV + 4 * W_C
N_EXPERTS = 16
EC_FACTOR = 2
D_EXPERT = 1024
N_MOD = 6
NORM_EPS = 1e-6
NEG_INF = -1e30
F32 = jnp.float32
BF16 = jnp.bfloat16

LANES = 128
VMEM_LIMIT = 56 * 1024 * 1024
LOG2E = 1.4426950408889634


def _cparams(sem):
    return pltpu.CompilerParams(dimension_semantics=sem, vmem_limit_bytes=VMEM_LIMIT)


def _silu(x):
    return x * (1.0 / (1.0 + jnp.exp(-x)))


MOD_TN = 1024


def _mod_kernel(c_ref, w_ref, b_ref, o_ref):
    c = _silu(c_ref[...]).astype(BF16)
    o_ref[...] = jnp.dot(c, w_ref[...].astype(BF16), preferred_element_type=F32) + b_ref[...]


def _modulation(c_rows, ada_w, ada_b):
    n_out = N_MOD * D_MODEL
    depth = ada_w.shape[0]
    return pl.pallas_call(
        _mod_kernel,
        grid=(depth, n_out // MOD_TN),
        in_specs=[pl.BlockSpec((8, D_MODEL), lambda l, j: (0, 0)),
                  pl.BlockSpec((None, D_MODEL, MOD_TN), lambda l, j: (l, 0, j)),
                  pl.BlockSpec((None, 1, MOD_TN), lambda l, j: (l, 0, j))],
        out_specs=pl.BlockSpec((None, 8, MOD_TN), lambda l, j: (l, 0, j)),
        out_shape=jax.ShapeDtypeStruct((depth, 8, n_out), F32),
        compiler_params=_cparams(("parallel", "parallel")),
        name="adaln_mod",
    )(c_rows, ada_w, ada_b)


PROJ_TN = 512
N_PROJ_TILES = D_MAIN // PROJ_TN


def _ada_norm_rows(x, gain, shift, scale):
    y = x * lax.rsqrt(jnp.mean(x * x, axis=-1, keepdims=True) + NORM_EPS)
    return (y * gain) * (1.0 + scale) + shift


def _head_norm(a, gain):
    return a * lax.rsqrt(jnp.mean(a * a, axis=-1, keepdims=True) + NORM_EPS) * gain


def _rope(a, cos, sin):
    lane = lax.broadcasted_iota(jnp.int32, a.shape, 1)
    partner = jnp.where((lane % 64) < 32, pltpu.roll(a, LANES - 32, 1), pltpu.roll(a, 32, 1))
    return a * cos + partner * sin


def _proj_kernel(x_ref, mod_ref, g_ref, w_ref, ws_ref, gains_ref, gp_ref, cos_ref, sin_ref,
                 o_ref, gates_ref, h_ref, *, rope, tm):
    j = pl.program_id(2)

    @pl.when(j == 0)
    def _():
        h = _ada_norm_rows(x_ref[...], g_ref[...], mod_ref[0:1, :], mod_ref[1:2, :])
        hb = h.astype(BF16)
        h_ref[...] = hb
        raw = jnp.dot(hb, ws_ref[...], preferred_element_type=F32)
        lane = lax.broadcasted_iota(jnp.int32, raw.shape, 1)
        beta = 1.0 / (1.0 + jnp.exp(-raw))
        z = raw + gp_ref[1:2, :]
        softplus = jnp.maximum(z, 0.0) + jnp.log1p(jnp.exp(-jnp.abs(z)))
        gates_ref[...] = jnp.where(lane < 2 * DN_HEADS, beta, -jnp.exp(gp_ref[0:1, :]) * softplus)

    acc = jnp.dot(h_ref[...], w_ref[...], preferred_element_type=F32)

    def normed(col, gi, with_rope):
        a = _head_norm(acc[:, col * LANES:(col + 1) * LANES], gains_ref[gi:gi + 1, :])
        if with_rope and rope:
            a = _rope(a, cos_ref[...], sin_ref[...])
        return a.astype(BF16)

    @pl.when(j < 2)
    def _():
        gain = gains_ref[pl.ds(j, 1), :]
        for col in range(4):
            o_ref[:, col * LANES:(col + 1) * LANES] = _head_norm(
                acc[:, col * LANES:(col + 1) * LANES], gain).astype(BF16)

    @pl.when(jnp.logical_or(j == 3, j == 4))
    def _():
        for col in range(4):
            o_ref[:, col * LANES:(col + 1) * LANES] = normed(col, 2, True)

    @pl.when(j == 5)
    def _():
        for col in range(2):
            o_ref[:, col * LANES:(col + 1) * LANES] = normed(col, 3, True)
        o_ref[:, 2 * LANES:] = acc[:, 2 * LANES:].astype(BF16)

    @pl.when(jnp.logical_or(j == 2, j > 5))
    def _():
        o_ref[...] = acc.astype(BF16)


def _project(x, mod, g_mix, w_main, w_small, gains, gate_params, cos_t, sin_t, *, rope, tm, per_batch_mod):
    B, N, _ = x.shape
    mod_idx = (lambda b, i, j: (b, 0, 0)) if per_batch_mod else (lambda b, i, j: (0, 0, 0))
    return pl.pallas_call(
        functools.partial(_proj_kernel, rope=rope, tm=tm),
        grid=(B, N // tm, N_PROJ_TILES),
        in_specs=[pl.BlockSpec((None, tm, D_MODEL), lambda b, i, j: (b, i, 0)),
                  pl.BlockSpec((None, 8, D_MODEL), mod_idx),
                  pl.BlockSpec((1, D_MODEL), lambda b, i, j: (0, 0)),
                  pl.BlockSpec((D_MODEL, PROJ_TN), lambda b, i, j: (0, j)),
                  pl.BlockSpec((D_MODEL, LANES), lambda b, i, j: (0, 0)),
                  pl.BlockSpec((8, LANES), lambda b, i, j: (0, 0)),
                  pl.BlockSpec((8, LANES), lambda b, i, j: (0, 0)),
                  pl.BlockSpec((tm, LANES), lambda b, i, j: (i, 0)),
                  pl.BlockSpec((tm, LANES), lambda b, i, j: (i, 0))],
        out_specs=[pl.BlockSpec((None, tm, PROJ_TN), lambda b, i, j: (b, i, j)),
                   pl.BlockSpec((None, tm, LANES), lambda b, i, j: (b, i, 0))],
        out_shape=[jax.ShapeDtypeStruct((B, N, D_MAIN), BF16),
                   jax.ShapeDtypeStruct((B, N, LANES), F32)],
        scratch_shapes=[pltpu.VMEM((tm, D_MODEL), BF16)],
        compiler_params=_cparams(("parallel", "parallel", "arbitrary")),
        name="in_proj",
    )(x, mod, g_mix, w_main, w_small, gains, gate_params, cos_t, sin_t)


SM_SCALE = HEAD_DIM ** -0.5
_NT = (((1,), (1,)), ((), ()))
COL_QA, COL_KA, COL_VA = 0, W_A // LANES, 2 * W_A // LANES
COL_QB = 3 * W_A // LANES
COL_KB = COL_QB + W_B // LANES
COL_VB = COL_KB + W_KV // LANES
COL_QC = COL_VB + W_KV // LANES
COL_GC = COL_QC + 3 * W_C // LANES


def _softmax_pv(score_blocks, value_blocks):
    m = functools.reduce(jnp.maximum, [jnp.max(s, axis=-1, keepdims=True) for s in score_blocks])
    ps = [jnp.exp2((s - m) * (SM_SCALE * LOG2E)) for s in score_blocks]
    o = functools.reduce(jnp.add, [jnp.dot(p.astype(BF16), _with_ones(v), preferred_element_type=F32)
                                   for p, v in zip(ps, value_blocks)])
    return o[:, :LANES] / o[:, LANES:LANES + 1]


def _with_ones(v):
    ones = jnp.where(lax.broadcasted_iota(jnp.int32, v.shape, 1) == 0, 1.0, 0.0).astype(v.dtype)
    return jnp.concatenate([v, ones], axis=1)


GQA_TQ = 512
GQA_GROUP = GQA_Q_HEADS // GQA_KV_HEADS


GQA_TK = 2048


def _gqa_kernel(q_ref, kl_ref, vl_ref, kc_ref, vc_ref, o_ref):
    n_lat = kl_ref.shape[0]
    c_exp = SM_SCALE * LOG2E
    for r in range(GQA_GROUP):
        q = q_ref[:, r * LANES:(r + 1) * LANES]
        m = l = acc = None
        for c0 in [None] + list(range(0, n_lat, GQA_TK)):
            k = kc_ref[...] if c0 is None else kl_ref[c0:c0 + GQA_TK, :]
            v = vc_ref[...] if c0 is None else vl_ref[c0:c0 + GQA_TK, :]
            v_ext = _with_ones(v)
            s = lax.dot_general(q, k, _NT, preferred_element_type=F32)
            m_c = jnp.max(s, axis=-1, keepdims=True)
            if m is None:
                m = m_c
                p = jnp.exp2((s - m) * c_exp)
                acc = jnp.dot(p.astype(BF16), v_ext, preferred_element_type=F32)
            else:
                m_new = jnp.maximum(m, m_c)
                alpha = jnp.exp2((m - m_new) * c_exp)
                p = jnp.exp2((s - m_new) * c_exp)
                acc = alpha * acc + jnp.dot(p.astype(BF16), v_ext, preferred_element_type=F32)
                m = m_new
        o_ref[:, r * LANES:(r + 1) * LANES] = (acc[:, :LANES] / acc[:, LANES:LANES + 1]).astype(BF16)


def _gqa_attention(p_lat, p_ctx):
    B, N, _ = p_lat.shape
    M = p_ctx.shape[1]
    qw = GQA_GROUP * LANES
    return pl.pallas_call(
        _gqa_kernel,
        grid=(B, GQA_KV_HEADS, N // GQA_TQ),
        in_specs=[pl.BlockSpec((None, GQA_TQ, qw), lambda b, g, i: (b, i, COL_QB // GQA_GROUP + g)),
                  pl.BlockSpec((None, N, LANES), lambda b, g, i: (b, 0, COL_KB + g)),
                  pl.BlockSpec((None, N, LANES), lambda b, g, i: (b, 0, COL_VB + g)),
                  pl.BlockSpec((None, M, LANES), lambda b, g, i: (b, 0, COL_KB + g)),
                  pl.BlockSpec((None, M, LANES), lambda b, g, i: (b, 0, COL_VB + g))],
        out_specs=pl.BlockSpec((None, GQA_TQ, qw), lambda b, g, i: (b, i, g)),
        out_shape=jax.ShapeDtypeStruct((B, N, W_B), BF16),
        compiler_params=_cparams(("parallel", "parallel", "parallel")),
        name="gqa_attention",
    )(p_lat, p_lat, p_lat, p_ctx, p_ctx)


def _ctx_attn_kernel(p_ref, oa_ref, ob_ref):
    def col(c):
        return p_ref[:, c * LANES:(c + 1) * LANES]
    for h in range(NA_HEADS):
        s = lax.dot_general(col(COL_QA + h), col(COL_KA + h), _NT, preferred_element_type=F32)
        oa_ref[:, h * LANES:(h + 1) * LANES] = _softmax_pv([s], [col(COL_VA + h)]).astype(BF16)
    for h in range(GQA_Q_HEADS):
        g = h // GQA_GROUP
        s = lax.dot_general(col(COL_QB + h), col(COL_KB + g), _NT, preferred_element_type=F32)
        ob_ref[:, h * LANES:(h + 1) * LANES] = _softmax_pv([s], [col(COL_VB + g)]).astype(BF16)


def _ctx_attention(p_ctx):
    B, M, _ = p_ctx.shape
    return pl.pallas_call(
        _ctx_attn_kernel,
        grid=(B,),
        in_specs=[pl.BlockSpec((None, M, D_MAIN), lambda b: (b, 0, 0))],
        out_specs=[pl.BlockSpec((None, M, W_A), lambda b: (b, 0, 0)),
                   pl.BlockSpec((None, M, W_B), lambda b: (b, 0, 0))],
        out_shape=[jax.ShapeDtypeStruct((B, M, W_A), BF16), jax.ShapeDtypeStruct((B, M, W_B), BF16)],
        compiler_params=_cparams(("parallel",)),
        name="ctx_attention",
    )(p_ctx)


NA_ROWS = 8
NA_KROWS = 16
NA_TQ = NA_ROWS * GRID_W
NA_TK = NA_KROWS * GRID_W


NA_HPS = 4


def _na_kernel(q_ref, k_ref, v_ref, kc_ref, vc_ref, bias_ref, o_ref, *, n_rows):
    g = pl.program_id(2)
    kbase = jnp.clip(g * NA_ROWS - NA_WIN_R // 2, 0, n_rows - NA_KROWS)
    off = pl.multiple_of(kbase * GRID_W, GRID_W)
    for h in range(NA_HPS):
        lanes = slice(h * LANES, (h + 1) * LANES)
        q = q_ref[:, lanes]
        k = k_ref[pl.ds(off, NA_TK), lanes]
        v = v_ref[pl.ds(off, NA_TK), lanes]
        s_loc = lax.dot_general(q, k, _NT, preferred_element_type=F32) + bias_ref[h]
        s_ctx = lax.dot_general(q, kc_ref[:, lanes], _NT, preferred_element_type=F32)
        o_ref[:, lanes] = _softmax_pv([s_loc, s_ctx], [v, vc_ref[:, lanes]]).astype(BF16)


def _na_bias_tables(rpb, n_rows):
    wr = min(NA_WIN_R, n_rows)
    qc = np.arange(GRID_W)
    cstart = np.clip(qc - NA_WIN_C // 2, 0, GRID_W - NA_WIN_C)
    in_col = (qc[None, :] >= cstart[:, None]) & (qc[None, :] < cstart[:, None] + NA_WIN_C)
    dc = np.clip(qc[None, :] - qc[:, None] + (NA_WIN_C - 1), 0, 2 * NA_WIN_C - 2)
    pick = (dc[..., None] == np.arange(2 * NA_WIN_C - 1)).astype(np.float32)
    cols = jnp.einsum('lhab,qkb->lhaqk', rpb.astype(F32), pick, precision=lax.Precision.HIGHEST)
    cols = jnp.where(in_col, cols / SM_SCALE, NEG_INF)
    masked = jnp.full(cols.shape[:2] + (GRID_W, GRID_W), NEG_INF, F32)
    tabs = []
    for g in (0, 1, n_rows // NA_ROWS - 1):
        r0 = g * NA_ROWS
        kbase = min(max(r0 - NA_WIN_R // 2, 0), n_rows - NA_KROWS)
        rows = []
        for r in range(NA_ROWS):
            qr = r0 + r
            rstart = min(max(qr - wr // 2, 0), n_rows - wr)
            blocks = []
            for d in range(NA_KROWS):
                kr = kbase + d
                blocks.append(cols[:, :, kr - qr + NA_WIN_R - 1] if rstart <= kr < rstart + wr else masked)
            rows.append(jnp.concatenate(blocks, axis=-1))
        tabs.append(jnp.concatenate(rows, axis=-2))
    return jnp.stack(tabs, axis=2)


def _na_attention(p_lat, p_ctx, bias_tabs, layer):
    B, N, _ = p_lat.shape
    M = p_ctx.shape[1]
    n_rows = N // GRID_W
    n_groups = n_rows // NA_ROWS
    variant = lambda g: jnp.where(g == 0, 0, jnp.where(g == n_groups - 1, 2, 1))
    hw = NA_HPS * LANES
    return pl.pallas_call(
        functools.partial(_na_kernel, n_rows=n_rows),
        grid=(B, NA_HEADS // NA_HPS, n_groups),
        in_specs=[pl.BlockSpec((None, NA_TQ, hw), lambda b, h, g: (b, g, COL_QA // NA_HPS + h)),
                  pl.BlockSpec((None, N, hw), lambda b, h, g: (b, 0, COL_KA // NA_HPS + h)),
                  pl.BlockSpec((None, N, hw), lambda b, h, g: (b, 0, COL_VA // NA_HPS + h)),
                  pl.BlockSpec((None, M, hw), lambda b, h, g: (b, 0, COL_KA // NA_HPS + h)),
                  pl.BlockSpec((None, M, hw), lambda b, h, g: (b, 0, COL_VA // NA_HPS + h)),
                  pl.BlockSpec((None, NA_HPS, None, NA_TQ, NA_TK), lambda b, h, g: (layer, h, variant(g), 0, 0))],
        out_specs=pl.BlockSpec((None, NA_TQ, hw), lambda b, h, g: (b, g, h)),
        out_shape=jax.ShapeDtypeStruct((B, N, W_A), BF16),
        compiler_params=_cparams(("parallel", "parallel", "parallel")),
        name="na_attention",
    )(p_lat, p_lat, p_lat, p_ctx, p_ctx, bias_tabs)


DN_L = 128
DN_HALF = DN_L // 2
DN_CHAINS = 2 * DN_HEADS
_TN = (((0,), (0,)), ((), ()))


def _dn_conv_kernel(xc_ref, xl_ref, w_ref, o_ref, pad_ref, *, n_ctx, n_lat):
    which = pl.program_id(1)
    half = DN_CONV // 2
    q_scale = jnp.where(which == 0, HEAD_DIM ** -0.5, 1.0)

    def run(x_ref, n, out_off):
        pad_ref[0:8, :] = jnp.zeros((8, LANES), F32)
        pad_ref[8 + n:16 + n, :] = jnp.zeros((8, LANES), F32)
        pad_ref[8:8 + n, :] = x_ref[...].astype(F32)
        chunk = min(n, 256)
        for c0 in range(0, n, chunk):
            y = jnp.zeros((chunk, LANES), F32)
            for i in range(DN_CONV):
                y = y + w_ref[i:i + 1, :] * pad_ref[8 + c0 + i - half:8 + c0 + i - half + chunk, :]
            y = _silu(y)
            yn = y * lax.rsqrt(jnp.sum(y * y, axis=-1, keepdims=True) + NORM_EPS) * q_scale
            o_ref[out_off + c0:out_off + c0 + chunk, :] = jnp.where(which < 2, yn, y)

    run(xc_ref, n_ctx, 0)
    run(xl_ref, n_lat, n_ctx)


def _dn_conv(p_ctx, p_lat, conv_w):
    B, N, _ = p_lat.shape
    M = p_ctx.shape[1]
    col = lambda b, s, h: (b, 0, COL_QC + s * DN_HEADS + h)
    return pl.pallas_call(
        functools.partial(_dn_conv_kernel, n_ctx=M, n_lat=N),
        grid=(B, 3, DN_HEADS),
        in_specs=[pl.BlockSpec((None, M, LANES), col),
                  pl.BlockSpec((None, N, LANES), col),
                  pl.BlockSpec((8, LANES), lambda b, s, h: (0, s * DN_HEADS + h))],
        out_specs=pl.BlockSpec((None, None, M + N, LANES), lambda b, s, h: (b, s, 0, h)),
        out_shape=jax.ShapeDtypeStruct((B, 3, M + N, W_C), F32),
        scratch_shapes=[pltpu.VMEM((N + 16, LANES), F32)],
        compiler_params=_cparams(("parallel", "parallel", "parallel")),
        name="dn_conv",
    )(p_ctx, p_lat, conv_w)


def _scan_incl(x, axis):
    idx = lax.broadcasted_iota(jnp.int32, x.shape, axis)
    s = 1
    while s < DN_L:
        x = x + jnp.where(idx >= s, pltpu.roll(x, s, axis), 0.0)
        s *= 2
    return x


def _dn_gate_sums(gates, gates_t):
    pf = _scan_incl(gates, 0)
    tot = pf[DN_L - 1:DN_L, :]
    pf_t = _scan_incl(gates_t, 1)
    tot_t = pf_t[:, DN_L - 1:DN_L]
    return (pf, tot - pf + gates), (pf_t, tot_t - pf_t + gates_t), tot


def _dn_chain_terms(h, d, gates, gates_t, sums):
    (cum, cum_t, tot) = sums
    lb = d * DN_HEADS + h
    lg = 2 * DN_HEADS + lb
    row = lax.broadcasted_iota(jnp.int32, (DN_L, DN_L), 0)
    col = lax.broadcasted_iota(jnp.int32, (DN_L, DN_L), 1)
    return (gates[:, lb:lb + 1], gates_t[lb:lb + 1, :], cum[d][:, lg:lg + 1], cum_t[d][lg:lg + 1, :],
            tot[:, lg:lg + 1], row, col)


def _dn_lower_form(h, d, gram, gates, gates_t, sums):
    beta_c, beta_r, g_c, g_r, _, row, col = _dn_chain_terms(h, d, gates, gates_t, sums)
    if d == 0:
        m = beta_c * gram * jnp.exp(jnp.where(row > col, g_c - g_r, NEG_INF))
    else:
        m = beta_r * gram * jnp.exp(jnp.where(row > col, g_r - g_c, NEG_INF))
    return jnp.where(row > col, m, 0.0)


def _dn_gram_kernel(k_ref, g_ref, gt_ref, m_ref):
    gates, gates_t = g_ref[...], gt_ref[...]
    sums = _dn_gate_sums(gates, gates_t)
    for h in range(DN_HEADS):
        kb = k_ref[:, h * LANES:(h + 1) * LANES].astype(BF16)
        gram = lax.dot_general(kb, kb, _NT, preferred_element_type=F32)
        for d in range(2):
            m_ref[d * DN_HEADS + h] = _dn_lower_form(h, d, gram, gates, gates_t, sums)


def _dn_gram(dnx, gates, gates_t):
    B, _, T, _ = dnx.shape
    n_sc = T // DN_L
    return pl.pallas_call(
        _dn_gram_kernel,
        grid=(B, n_sc),
        in_specs=[pl.BlockSpec((None, None, DN_L, W_C), lambda b, s: (b, 1, s, 0)),
                  pl.BlockSpec((None, DN_L, LANES), lambda b, s: (b, s, 0)),
                  pl.BlockSpec((None, 16, DN_L), lambda b, s: (b, 0, s))],
        out_specs=pl.BlockSpec((None, None, DN_CHAINS, DN_L, DN_L), lambda b, s: (b, s, 0, 0, 0)),
        out_shape=jax.ShapeDtypeStruct((B, n_sc, DN_CHAINS, DN_L, DN_L), F32),
        compiler_params=_cparams(("parallel", "parallel")),
        name="dn_gram",
    )(dnx, gates, gates_t)


TRI_GROUP = LANES // 2


def _tri_inv_kernel(m_ref, t_ref, a_scr, t_scr):
    n = DN_HALF
    lane_n = lax.broadcasted_iota(jnp.int32, (n, LANES), 1)
    for i in range(n):
        rt = jnp.concatenate([m_ref[:, i, :], m_ref[:, n + i, :]], axis=0).T
        a_scr[i * n:(i + 1) * n, :] = jnp.where(lane_n < n, rt[:n], rt[n:])

    t_scr[...] = jnp.zeros(t_scr.shape, F32)
    col_id = lax.broadcasted_iota(jnp.int32, (n, LANES), 0)

    for ib in range(n // 8):
        def row_step(ii, carry, ib=ib):
            i = ib * 8 + ii
            acc = jnp.where(col_id == i, 1.0, 0.0)
            for jb in range(ib + 1):
                h = 8 * (jb + 1)
                part = acc[:h]
                for jj in range(8):
                    j = jb * 8 + jj
                    a = a_scr[pl.ds(i * n + j, 1), :]
                    part = part - a * t_scr[j * n:j * n + h, :]
                acc = part if h == n else jnp.concatenate([part, acc[h:]], axis=0)
            t_scr[pl.ds(pl.multiple_of(i * n, n), n), :] = acc
            return carry

        lax.fori_loop(0, 8, row_step, 0)

    for i in range(n):
        slab = t_scr[i * n:(i + 1) * n, :]
        st = jnp.concatenate([slab, slab], axis=0).T
        t_ref[:, i, :] = jnp.where(lane_n < n, st[:n], 0.0)
        t_ref[:, n + i, :] = jnp.where(lane_n >= n, st[n:], 0.0)


def _tri_inv(m_low):
    n = m_low.shape[0]
    assert n % TRI_GROUP == 0, n
    return pl.pallas_call(
        _tri_inv_kernel,
        grid=(n // TRI_GROUP,),
        in_specs=[pl.BlockSpec((TRI_GROUP, DN_L, DN_L), lambda g: (g, 0, 0))],
        out_specs=pl.BlockSpec((TRI_GROUP, DN_L, DN_L), lambda g: (g, 0, 0)),
        out_shape=jax.ShapeDtypeStruct(m_low.shape, F32),
        scratch_shapes=[pltpu.VMEM((DN_HALF * DN_HALF, LANES), F32), pltpu.VMEM((DN_HALF * DN_HALF, LANES), F32)],
        compiler_params=_cparams(("parallel",)),
        name="dn_tri_inv",
    )(m_low)


def _dn_prep_kernel(q_ref, k_ref, v_ref, g_ref, gt_ref, m_ref, t_ref, u_ref, ws_ref, wv_ref, gl_ref):
    gates, gates_t = g_ref[...], gt_ref[...]
    sums = _dn_gate_sums(gates, gates_t)
    H = DN_HALF
    for h in range(DN_HEADS):
        q = q_ref[:, h * LANES:(h + 1) * LANES]
        k = k_ref[:, h * LANES:(h + 1) * LANES]
        v = v_ref[:, h * LANES:(h + 1) * LANES]
        qk_raw = lax.dot_general(q.astype(BF16), k.astype(BF16), _NT, preferred_element_type=F32)
        for d in range(2):
            c = d * DN_HEADS + h
            beta_c, _, g_c, g_r, g_tot, row, col = _dn_chain_terms(h, d, gates, gates_t, sums)
            causal = (row >= col) if d == 0 else (row <= col)
            decay = jnp.exp(jnp.where(causal, g_c - g_r, NEG_INF))
            eg = jnp.exp(g_c)
            rhs = jnp.concatenate([v * beta_c, k * (beta_c * eg)], axis=1)
            t_bd = t_ref[c].astype(BF16)
            m21 = jnp.where(jnp.logical_and(row >= H, col < H), m_ref[c], 0.0).astype(BF16)
            apply = (lambda a, b: jnp.dot(a, b.astype(BF16), preferred_element_type=F32)) if d == 0 else (
                lambda a, b: lax.dot_general(a, b.astype(BF16), _TN, preferred_element_type=F32))
            first = apply(t_bd, rhs)
            sol = apply(t_bd, rhs - apply(m21, first))
            u_ref[c] = sol[:, :LANES]
            ws_ref[c, :DN_L, :] = sol[:, LANES:].astype(BF16)
            ws_ref[c, DN_L:, :] = (q * eg).astype(BF16)
            wv_ref[c, :DN_L, :] = (qk_raw * decay).astype(BF16)
            wv_ref[c, DN_L:, :] = (k * jnp.exp(g_tot - g_c)).T.astype(BF16)
            gl_ref[c] = jnp.broadcast_to(jnp.exp(g_tot), (8, LANES))


def _dn_prep(dnx, gates, gates_t, m_low, t_inv):
    B, _, T, _ = dnx.shape
    n_sc = T // DN_L
    blk = pl.BlockSpec((None, None, DN_CHAINS, DN_L, DN_L), lambda b, s: (b, s, 0, 0, 0))
    blk2 = pl.BlockSpec((None, None, DN_CHAINS, 2 * DN_L, DN_L), lambda b, s: (b, s, 0, 0, 0))
    stacked = jax.ShapeDtypeStruct((B, n_sc, DN_CHAINS, 2 * DN_L, DN_L), BF16)
    return pl.pallas_call(
        _dn_prep_kernel,
        grid=(B, n_sc),
        in_specs=[pl.BlockSpec((None, None, DN_L, W_C), lambda b, s: (b, 0, s, 0)),
                  pl.BlockSpec((None, None, DN_L, W_C), lambda b, s: (b, 1, s, 0)),
                  pl.BlockSpec((None, None, DN_L, W_C), lambda b, s: (b, 2, s, 0)),
                  pl.BlockSpec((None, DN_L, LANES), lambda b, s: (b, s, 0)),
                  pl.BlockSpec((None, 16, DN_L), lambda b, s: (b, 0, s)),
                  blk, blk],
        out_specs=[blk, blk2, blk2,
                   pl.BlockSpec((None, None, DN_CHAINS, 8, LANES), lambda b, s: (b, s, 0, 0, 0))],
        out_shape=[jax.ShapeDtypeStruct((B, n_sc, DN_CHAINS, DN_L, DN_L), F32), stacked, stacked,
                   jax.ShapeDtypeStruct((B, n_sc, DN_CHAINS, 8, LANES), F32)],
        compiler_params=_cparams(("parallel", "parallel")),
        name="dn_prep",
    )(dnx, dnx, dnx, gates, gates_t, m_low, t_inv)


def _dn_scan_kernel(uf, wsf, wvf, glf, ub, wsb, wvb, glb, of_ref, ob_ref, s_ref):
    @pl.when(pl.program_id(1) == 0)
    def _():
        s_ref[...] = jnp.zeros(s_ref.shape, F32)

    for d, (u, ws, wv, gl, o_ref) in enumerate(((uf, wsf, wvf, glf, of_ref), (ub, wsb, wvb, glb, ob_ref))):
        for h in range(DN_HEADS):
            c = d * DN_HEADS + h
            s = s_ref[c]
            on_s = jnp.dot(ws[h], s.astype(BF16), preferred_element_type=F32)
            v_new = u[h] - on_s[:DN_L]
            on_v = jnp.dot(wv[h], v_new.astype(BF16), preferred_element_type=F32)
            s_ref[c] = s * gl[h][0:1, :] + on_v[DN_L:]
            o_ref[:, h * LANES:(h + 1) * LANES] = on_s[DN_L:] + on_v[:DN_L]


def _dn_scan(u, ws, wv, gl, n_ctx_chunks):
    B, n_sc = u.shape[0], u.shape[1]
    fwd = lambda b, s: s
    bwd = lambda b, s: jnp.where(s < n_ctx_chunks, n_ctx_chunks - 1 - s, n_sc - 1 + n_ctx_chunks - s)
    specs = []
    for d, walk in ((0, fwd), (1, bwd)):
        for rows in (DN_L, 2 * DN_L, 2 * DN_L, 8):
            specs.append(pl.BlockSpec((None, None, DN_HEADS, rows, LANES),
                                      functools.partial(lambda b, s, d, walk: (b, walk(b, s), d, 0, 0), d=d, walk=walk)))
    out = jax.ShapeDtypeStruct((B, n_sc * DN_L, W_C), F32)
    return pl.pallas_call(
        _dn_scan_kernel,
        grid=(B, n_sc),
        in_specs=specs,
        out_specs=[pl.BlockSpec((None, DN_L, W_C), lambda b, s: (b, fwd(b, s), 0)),
                   pl.BlockSpec((None, DN_L, W_C), lambda b, s: (b, bwd(b, s), 0))],
        out_shape=[out, out],
        scratch_shapes=[pltpu.VMEM((DN_CHAINS, DN_L, DN_L), F32)],
        compiler_params=_cparams(("parallel", "arbitrary")),
        name="dn_scan",
    )(u, ws, wv, gl, u, ws, wv, gl)


def _deltanet(p_ctx, p_lat, gates_ctx, gates_lat, conv_w):
    B = p_lat.shape[0]
    n_ctx_chunks = p_ctx.shape[1] // DN_L
    dnx = _dn_conv(p_ctx, p_lat, conv_w)
    gates = jnp.concatenate([gates_ctx, gates_lat], axis=1)
    gates_t = jnp.swapaxes(gates[:, :, :16], 1, 2)
    m_low = _dn_gram(dnx, gates, gates_t)
    t_inv = _tri_inv(m_low.reshape(-1, DN_L, DN_L)).reshape(m_low.shape)
    u, ws, wv, gl = _dn_prep(dnx, gates, gates_t, m_low, t_inv)
    return _dn_scan(u, ws, wv, gl, n_ctx_chunks)


OUT_TM = 256


def _out_proj_kernel(oa_ref, ob_ref, of_ref, obk_ref, gc_ref, x_ref, mod_ref, gain_ref, w_ref, o_ref, o2_ref):
    o_dn = of_ref[...] + obk_ref[...]
    acc = jnp.dot(oa_ref[...], w_ref[0:W_A, :], preferred_element_type=F32)
    acc = acc + jnp.dot(ob_ref[...], w_ref[W_A:W_A + W_B, :], preferred_element_type=F32)
    for h in range(DN_HEADS):
        oh = _head_norm(o_dn[:, h * LANES:(h + 1) * LANES], gain_ref[0:1, :])
        gate = gc_ref[:, h * LANES:(h + 1) * LANES].astype(F32)
        oc = (oh * _silu(gate)).astype(BF16)
        lo = W_A + W_B + h * LANES
        acc = acc + jnp.dot(oc, w_ref[lo:lo + LANES, :], preferred_element_type=F32)
    y = x_ref[...] + mod_ref[2:3, :] * acc
    o_ref[...] = y
    o2_ref[...] = y


def _out_project(oa, ob, o_f, o_b, proj, x, mod, out_gain, w_out, *, dn_row_off, per_batch_mod):
    B, N, _ = x.shape
    tm = OUT_TM
    off = dn_row_off // tm
    mod_idx = (lambda b, i: (b, 0, 0)) if per_batch_mod else (lambda b, i: (0, 0, 0))
    return pl.pallas_call(
        _out_proj_kernel,
        grid=(B, N // tm),
        in_specs=[pl.BlockSpec((None, tm, W_A), lambda b, i: (b, i, 0)),
                  pl.BlockSpec((None, tm, W_B), lambda b, i: (b, i, 0)),
                  pl.BlockSpec((None, tm, W_C), lambda b, i: (b, i + off, 0)),
                  pl.BlockSpec((None, tm, W_C), lambda b, i: (b, i + off, 0)),
                  pl.BlockSpec((None, tm, W_C), lambda b, i: (b, i, COL_GC * LANES // W_C)),
                  pl.BlockSpec((None, tm, D_MODEL), lambda b, i: (b, i, 0)),
                  pl.BlockSpec((None, 8, D_MODEL), mod_idx),
                  pl.BlockSpec((8, LANES), lambda b, i: (0, 0)),
                  pl.BlockSpec((MIX_WIDTH, D_MODEL), lambda b, i: (0, 0))],
        out_specs=[pl.BlockSpec((None, tm, D_MODEL), lambda b, i: (b, i, 0))] * 2,
        out_shape=[jax.ShapeDtypeStruct(x.shape, F32)] * 2,
        compiler_params=_cparams(("parallel", "parallel")),
        name="out_proj",
    )(oa, ob, o_f, o_b, proj, x, mod, out_gain, w_out)


ROUTER_TM = 512


def _router_kernel(x_ref, mod_ref, g_ref, rw_ref, aff_ref):
    h = _ada_norm_rows(x_ref[...], g_ref[...], mod_ref[3:4, :], mod_ref[4:5, :])
    h_hi = h.astype(BF16)
    h_lo = (h - h_hi.astype(F32)).astype(BF16)
    both = jnp.dot(h_hi, rw_ref[...], preferred_element_type=F32)
    logits = (both[:, :LANES] + both[:, LANES:]
              + jnp.dot(h_lo, rw_ref[:, :LANES], preferred_element_type=F32))
    lane = lax.broadcasted_iota(jnp.int32, logits.shape, 1)
    logits = jnp.where(lane < N_EXPERTS, logits, NEG_INF)
    p = jnp.exp(logits - jnp.max(logits, axis=-1, keepdims=True))
    aff_ref[...] = p / jnp.sum(p, axis=-1, keepdims=True)


def _router(x, mod, g_ffn, router_w, *, per_batch_mod):
    B, N, _ = x.shape
    tm = min(ROUTER_TM, N)
    mod_idx = (lambda b, i: (b, 0, 0)) if per_batch_mod else (lambda b, i: (0, 0, 0))
    return pl.pallas_call(
        _router_kernel,
        grid=(B, N // tm),
        in_specs=[pl.BlockSpec((None, tm, D_MODEL), lambda b, i: (b, i, 0)),
                  pl.BlockSpec((None, 8, D_MODEL), mod_idx),
                  pl.BlockSpec((1, D_MODEL), lambda b, i: (0, 0)),
                  pl.BlockSpec((D_MODEL, 2 * LANES), lambda b, i: (0, 0))],
        out_specs=pl.BlockSpec((None, tm, LANES), lambda b, i: (b, i, 0)),
        out_shape=jax.ShapeDtypeStruct((B, N, LANES), F32),
        compiler_params=_cparams(("parallel", "parallel")),
        name="moe_router",
    )(x, mod, g_ffn, router_w)


def _select_kernel(aff_ref, out_ref, afft_ref, slot_ref, vals_ref, *, n_tok, cap):
    n_blk = n_tok // LANES
    for t in range(n_blk):
        afft_ref[:, t * LANES:(t + 1) * LANES] = aff_ref[t * LANES:(t + 1) * LANES, :].T[:N_EXPERTS, :]
    afft = afft_ref[...]
    count = lambda m: jnp.sum(jnp.where(m, 1.0, 0.0), axis=1, keepdims=True)

    def thr_step(i, t):
        cand = t | (jnp.int32(1) << (30 - i))
        return jnp.where(count(afft >= pltpu.bitcast(cand, F32)) >= cap, cand, t)
    thr = pltpu.bitcast(lax.fori_loop(0, 31, thr_step, jnp.zeros((N_EXPERTS, 1), jnp.int32)), F32)
    above = afft > thr
    tie = afft == thr
    need = cap - count(above)
    tok = lax.broadcasted_iota(jnp.int32, (N_EXPERTS, n_tok), 1)
    n_bits = n_tok.bit_length()

    def cut_step(i, cut):
        cand = cut + (jnp.int32(1) << (n_bits - 1 - i))
        ok = jnp.logical_and(cand <= n_tok, count(jnp.logical_and(tie, tok < cand)) <= need)
        return jnp.where(ok, cand, cut)
    cut = lax.fori_loop(0, n_bits, cut_step, jnp.zeros((N_EXPERTS, 1), jnp.int32))
    sel = jnp.logical_or(above, jnp.logical_and(tie, tok < cut))

    r_i = lax.broadcasted_iota(jnp.int32, (LANES, LANES), 0)
    c_i = lax.broadcasted_iota(jnp.int32, (LANES, LANES), 1)
    tri = jnp.where(r_i <= c_i, 1.0, 0.0).astype(BF16)
    base = jnp.zeros((N_EXPERTS, 1), F32)
    for t in range(n_blk):
        s_blk = sel[:, t * LANES:(t + 1) * LANES]
        ones = jnp.where(s_blk, 1.0, 0.0)
        incl = jnp.dot(ones.astype(BF16), tri, preferred_element_type=F32)
        slot_ref[:, t * LANES:(t + 1) * LANES] = jnp.where(s_blk, base + incl - 1.0, -1.0)
        base = base + incl[:, LANES - 1:LANES]

    a = aff_ref[...]
    hi = a.astype(BF16).astype(F32)
    mid = (a - hi).astype(BF16).astype(F32)
    lo = a - hi - mid
    lane_v = lax.broadcasted_iota(jnp.int32, (n_tok, LANES), 1)
    tok_v = lax.broadcasted_iota(jnp.int32, (n_tok, LANES), 0)
    vals = hi + pltpu.roll(mid, N_EXPERTS, 1) + pltpu.roll(lo, 2 * N_EXPERTS, 1)
    vals = jnp.where(lane_v == 3 * N_EXPERTS, (tok_v // 64).astype(F32), vals)
    vals = jnp.where(lane_v == 3 * N_EXPERTS + 1, (tok_v % 64).astype(F32), vals)
    vals_ref[...] = vals.astype(BF16)

    chunk = min(cap, LANES)
    lane_o = lax.broadcasted_iota(jnp.int32, (chunk, LANES), 1)
    slot_id = lax.broadcasted_iota(jnp.int32, (chunk, 1), 0).astype(F32)

    def expert(e, carry):
        slot_e = slot_ref[pl.ds(e, 1), :]
        for c0 in range(0, cap, chunk):
            onehot = jnp.where(slot_e == slot_id + float(c0), 1.0, 0.0).astype(BF16)
            res = jnp.dot(onehot, vals_ref[...], preferred_element_type=F32)
            mine = jnp.logical_and(lane_o % N_EXPERTS == e, lane_o < 3 * N_EXPERTS)
            gate_c = jnp.sum(jnp.where(mine, res, 0.0), axis=1, keepdims=True)
            idx_c = jnp.sum(jnp.where(lane_o == 3 * N_EXPERTS, res * 64.0,
                                      jnp.where(lane_o == 3 * N_EXPERTS + 1, res, 0.0)), axis=1, keepdims=True)
            out_ref[e, c0:c0 + chunk, :] = jnp.where(lane_o == 0, idx_c, jnp.where(lane_o == 1, gate_c, 0.0))
        return carry

    lax.fori_loop(0, N_EXPERTS, expert, 0)


def _select(aff, cap):
    S, n_tok, _ = aff.shape
    return pl.pallas_call(
        functools.partial(_select_kernel, n_tok=n_tok, cap=cap),
        grid=(S,),
        in_specs=[pl.BlockSpec((None, n_tok, LANES), lambda s: (s, 0, 0))],
        out_specs=pl.BlockSpec((None, N_EXPERTS, cap, LANES), lambda s: (s, 0, 0, 0)),
        out_shape=jax.ShapeDtypeStruct((S, N_EXPERTS, cap, LANES), F32),
        scratch_shapes=[pltpu.VMEM((N_EXPERTS, n_tok), F32), pltpu.VMEM((N_EXPERTS, n_tok), F32),
                        pltpu.VMEM((n_tok, LANES), BF16)],
        compiler_params=_cparams(("parallel",)),
        name="moe_select",
    )(aff)


FFN_TF = 256


def _moe_ffn_kernel(*refs, shapes, n_f):
    n = len(shapes)
    per_stream = [refs[3 * i:3 * i + 3] for i in range(n)]
    g_ref, wg_ref, wu_ref, wd_ref = refs[3 * n:3 * n + 4]
    x_hbms = refs[3 * n + 4:5 * n + 4:2]
    acc_hbms = refs[5 * n + 4:6 * n + 4]
    stage = refs[6 * n + 4]
    bufs = [refs[6 * n + 5 + 2 * i:6 * n + 7 + 2 * i] for i in range(n)]
    sem = refs[-1]
    f = pl.program_id(1)
    sets = [(i, s) for i, (n_sets, _) in enumerate(shapes) for s in range(n_sets)]

    def row_copies(i, s, src, dst, scatter):
        idx_ref, rows = per_stream[i][0], shapes[i][1]

        def issue(i, c):
            for k in range(8):
                r = i * 8 + k
                t = idx_ref[s, 0, r]
                if scatter:
                    pltpu.make_async_copy(src.at[pl.ds(r, 1), :], dst.at[pl.ds(t, 1), :], sem).start(priority=k % 2)
                else:
                    pltpu.make_async_copy(src.at[pl.ds(t, 1), :], dst.at[pl.ds(r, 1), :], sem).start(priority=k % 2)
            return c
        lax.fori_loop(0, rows // 8, issue, 0)
        whole = stage.at[pl.ds(0, rows), :]
        pltpu.make_async_copy(whole, whole, sem).wait()

    @pl.when(f == 0)
    def _():
        for i, s in sets:
            mod_ref, rows = per_stream[i][2], shapes[i][1]
            row_copies(i, s, x_hbms[i], stage, False)
            h = _ada_norm_rows(stage[0:rows, :], g_ref[...], mod_ref[s, 3:4, :], mod_ref[s, 4:5, :])
            bufs[i][0][s] = h.astype(BF16)

    wg = wg_ref[...].astype(BF16)
    wu = wu_ref[...].astype(BF16)
    wd = wd_ref[...].astype(BF16)
    for i, s in sets:
        hbuf, ybuf = bufs[i]
        hb = hbuf[s]
        a = jnp.dot(hb, wg, preferred_element_type=F32)
        u = jnp.dot(hb, wu, preferred_element_type=F32)
        y = jnp.dot((_silu(a) * u).astype(BF16), wd, preferred_element_type=F32)

        @pl.when(f == 0)
        def _():
            ybuf[s] = y

        @pl.when(f > 0)
        def _():
            ybuf[s] = ybuf[s] + y

    @pl.when(f == n_f - 1)
    def _():
        for i, s in sets:
            (_, gate_ref, mod_ref), rows = per_stream[i], shapes[i][1]
            row_copies(i, s, acc_hbms[i], stage, False)
            stage[0:rows, :] = stage[0:rows, :] + (mod_ref[s, 5:6, :] * gate_ref[s]) * bufs[i][1][s]
            row_copies(i, s, stage, acc_hbms[i], True)


def _moe_ffn(streams, g_ffn, w_gate, w_up, w_down, layer):
    n = len(streams)
    n_f = D_EXPERT // FFN_TF
    shapes = tuple((s[0].shape[1], s[0].shape[3]) for s in streams)
    in_specs, args = [], []
    for (idx, gate, mod, _, _), (S, rows) in zip(streams, shapes):
        in_specs += [pl.BlockSpec((None, S, 1, rows), lambda e, f: (e, 0, 0, 0), memory_space=pltpu.SMEM),
                     pl.BlockSpec((None, S, rows, 1), lambda e, f: (e, 0, 0, 0)),
                     pl.BlockSpec((S, 8, D_MODEL), lambda e, f: (0, 0, 0))]
        args += [idx, gate, mod]
    in_specs += [pl.BlockSpec((1, D_MODEL), lambda e, f: (0, 0)),
                 pl.BlockSpec((None, None, D_MODEL, FFN_TF), lambda e, f: (layer, e, 0, f)),
                 pl.BlockSpec((None, None, D_MODEL, FFN_TF), lambda e, f: (layer, e, 0, f)),
                 pl.BlockSpec((None, None, FFN_TF, D_MODEL), lambda e, f: (layer, e, f, 0))]
    args += [g_ffn, w_gate, w_up, w_down]
    for _, _, _, x2d, acc2d in streams:
        in_specs += [pl.BlockSpec(memory_space=pl.ANY)] * 2
        args += [x2d, acc2d]
    scratch = [pltpu.VMEM((max(r for _, r in shapes), D_MODEL), F32)]
    for S, rows in shapes:
        scratch += [pltpu.VMEM((S, rows, D_MODEL), BF16), pltpu.VMEM((S, rows, D_MODEL), F32)]
    return pl.pallas_call(
        functools.partial(_moe_ffn_kernel, shapes=shapes, n_f=n_f),
        grid=(N_EXPERTS, n_f),
        in_specs=in_specs,
        out_specs=[pl.BlockSpec(memory_space=pl.ANY)] * n,
        out_shape=[jax.ShapeDtypeStruct(s[4].shape, F32) for s in streams],
        scratch_shapes=scratch + [pltpu.SemaphoreType.DMA(())],
        input_output_aliases={3 * n + 4 + 2 * i + 1: i for i in range(n)},
        compiler_params=_cparams(("arbitrary", "arbitrary")),
        name="moe_ffn",
    )(*args)


def _route(x, mod, g_ffn, router_w, *, per_batch_mod):
    B, N, _ = x.shape
    cap = (EC_FACTOR * N) // N_EXPERTS
    aff = _router(x, mod, g_ffn, router_w, per_batch_mod=per_batch_mod)
    chosen = _select(aff, cap)
    rows_of = jnp.arange(B, dtype=jnp.int32)[:, None, None] * N
    idx = chosen[..., 0].astype(jnp.int32) + rows_of
    gate = chosen[..., 1]
    if per_batch_mod:
        idx = jnp.swapaxes(idx, 0, 1)[:, :, None, :]
        gate = jnp.swapaxes(gate, 0, 1)[..., None]
    else:
        idx = jnp.swapaxes(idx, 0, 1).reshape(N_EXPERTS, 1, 1, B * cap)
        gate = jnp.swapaxes(gate, 0, 1).reshape(N_EXPERTS, 1, B * cap, 1)
    return idx, gate


def _expert_choice(pairs, mods, g_ffn, router_w, w_gate, w_up, w_down, layer):
    streams = []
    for i, ((x, x_acc), mod) in enumerate(zip(pairs, mods)):
        idx, gate = _route(x, mod, g_ffn, router_w, per_batch_mod=(i == 0))
        flat = lambda a: a.reshape(-1, D_MODEL)
        streams.append((idx, gate, mod, flat(x), flat(x_acc)))
    outs = _moe_ffn(streams, g_ffn, w_gate, w_up, w_down, layer)
    return [o.reshape(p[0].shape) for o, p in zip(outs, pairs)]


def _rope_tables(n_tok):
    t = jnp.arange(n_tok, dtype=jnp.int32)
    pos = jnp.stack([t // GRID_W, t % GRID_W], axis=-1).astype(F32)
    n_freq = HEAD_DIM // 4
    freqs = ROPE_THETA ** (-jnp.arange(n_freq, dtype=F32) / n_freq)
    ang = pos[:, :, None] * freqs
    cos, sin = jnp.cos(ang), jnp.sin(ang)
    cos_t = jnp.concatenate([cos[:, 0], cos[:, 0], cos[:, 1], cos[:, 1]], axis=-1)
    sin_t = jnp.concatenate([-sin[:, 0], sin[:, 0], -sin[:, 1], sin[:, 1]], axis=-1)
    return cos_t, sin_t


def _pad_rows(a, rows=8):
    return jnp.pad(a, ((0, rows - a.shape[0]), (0, 0)))


def _split_bf16(w):
    hi = w.astype(BF16)
    return jnp.concatenate([hi, (w - hi.astype(F32)).astype(BF16)], axis=-1)


def _layer_weights(p):
    w_in = p["w_in"]
    gate_lanes = jnp.concatenate([jnp.zeros((2, 2 * DN_HEADS), F32),
                                  jnp.stack([p["dn_a_log"].reshape(-1), p["dn_dt_bias"].reshape(-1)])], axis=1)
    return {
        "g_mix": p["norm_mix"].reshape(1, D_MODEL),
        "g_ffn": p["norm_ffn"].reshape(1, D_MODEL),
        "w_main": w_in[:, :D_MAIN].astype(BF16),
        "w_small": jnp.pad(w_in[:, D_MAIN:], ((0, 0), (0, LANES - 4 * DN_HEADS))).astype(BF16),
        "gains": _pad_rows(jnp.concatenate([p["na_qk_gain"], p["gqa_qk_gain"]], axis=0)),
        "gate_params": _pad_rows(jnp.pad(gate_lanes, ((0, 0), (0, LANES - 4 * DN_HEADS)))),
        "w_out": p["w_out"].astype(BF16),
        "conv_w": _pad_rows(p["dn_conv"]),
        "out_gain": _pad_rows(p["dn_out_gain"].reshape(1, HEAD_DIM)),
        "router_w": _split_bf16(jnp.pad(p["router_w"], ((0, 0), (0, LANES - N_EXPERTS)))),
    }


def _hybrid_layer(x, ctx, mod_lat, mod_ctx, p, experts, na_bias, layer, tabs, need_ctx_out):
    W = _layer_weights(p)
    M = ctx.shape[1]
    proj = functools.partial(_project, g_mix=W["g_mix"], w_main=W["w_main"], w_small=W["w_small"],
                             gains=W["gains"], gate_params=W["gate_params"])
    p_lat, gates_lat = proj(x, mod_lat, cos_t=tabs[0], sin_t=tabs[1], rope=True, tm=1024, per_batch_mod=True)
    p_ctx, gates_ctx = proj(ctx, mod_ctx, cos_t=tabs[0][:M], sin_t=tabs[1][:M], rope=False, tm=M,
                            per_batch_mod=False)
    oa = _na_attention(p_lat, p_ctx, na_bias, layer)
    ob = _gqa_attention(p_lat, p_ctx)
    o_f, o_b = _deltanet(p_ctx, p_lat, gates_ctx, gates_lat, W["conv_w"])
    moe = functools.partial(_expert_choice, g_ffn=W["g_ffn"], router_w=W["router_w"], w_gate=experts[0],
                            w_up=experts[1], w_down=experts[2], layer=layer)
    x_pair = _out_project(oa, ob, o_f, o_b, p_lat, x, mod_lat, W["out_gain"], W["w_out"],
                          dn_row_off=M, per_batch_mod=True)
    if not need_ctx_out:
        return moe([x_pair], [mod_lat])[0], None
    oa_c, ob_c = _ctx_attention(p_ctx)
    ctx_pair = _out_project(oa_c, ob_c, o_f, o_b, p_ctx, ctx, mod_ctx, W["out_gain"], W["w_out"],
                            dn_row_off=0, per_batch_mod=False)
    x, ctx = moe([x_pair, ctx_pair], [mod_lat, mod_ctx])
    return x, ctx


def kernel(x, c, ctx, c_ctx, ada_w, ada_b, norm_mix, norm_ffn, w_in, w_out, na_qk_gain, na_rpb, gqa_qk_gain, dn_conv, dn_a_log, dn_dt_bias, dn_out_gain, router_w, exp_w_gate, exp_w_up, exp_w_down):
    B = x.shape[0]
    depth = ada_w.shape[0]
    c_rows = _pad_rows(jnp.concatenate([c, c_ctx[None, :]], axis=0))
    mods = _modulation(c_rows, ada_w, ada_b.reshape(depth, 1, -1)).reshape(depth, 8, N_MOD, D_MODEL)
    mods = jnp.pad(mods, ((0, 0), (0, 0), (0, 8 - N_MOD), (0, 0)))
    tabs = _rope_tables(x.shape[1])
    params = dict(norm_mix=norm_mix, norm_ffn=norm_ffn, w_in=w_in, w_out=w_out, na_qk_gain=na_qk_gain,
                  gqa_qk_gain=gqa_qk_gain, dn_conv=dn_conv, dn_a_log=dn_a_log,
                  dn_dt_bias=dn_dt_bias, dn_out_gain=dn_out_gain, router_w=router_w)
    experts = (exp_w_gate, exp_w_up, exp_w_down)
    na_bias = _na_bias_tables(na_rpb, x.shape[1] // GRID_W)
    for l in range(depth):
        p = {k: v[l] for k, v in params.items()}
        x, ctx = _hybrid_layer(x, ctx, mods[l, :B], mods[l, B:B + 1], p, experts, na_bias, l, tabs,
                               need_ctx_out=l < depth - 1)
    return x
```
